```python
import math
import jax, jax.numpy as jnp
from jax import lax
import numpy as np

D_MODEL = 4096
BATCH = 2
SEQ = 4096
DEPTH = 1

HEAD_DIM = 128
MIX_WIDTH = D_MODEL
HGRN_WIDTH = MIX_WIDTH // 2
GDN_WIDTH = MIX_WIDTH - HGRN_WIDTH
HGRN_HEADS = HGRN_WIDTH // HEAD_DIM
GDN_V_HEADS = GDN_WIDTH // HEAD_DIM
GDN_K_HEADS = GDN_V_HEADS // 2
GDN_K_WIDTH = GDN_K_HEADS * HEAD_DIM
GDN_QKV_WIDTH = 2 * GDN_K_WIDTH + GDN_WIDTH
GDN_CONV_WIDTH = 4
FFN_CONV_WIDTH = 3
D_FF = ((8 * D_MODEL // 3 + 255) // 256) * 256
CHUNK = 64
EPS = 1e-6
PROJ_WIDTH = 4 * HGRN_WIDTH + GDN_QKV_WIDTH + GDN_WIDTH + 2 * GDN_V_HEADS
SPLIT_OFFSETS = (
    HGRN_WIDTH,
    2 * HGRN_WIDTH,
    3 * HGRN_WIDTH,
    4 * HGRN_WIDTH,
    4 * HGRN_WIDTH + GDN_QKV_WIDTH,
    4 * HGRN_WIDTH + GDN_QKV_WIDTH + GDN_WIDTH,
    4 * HGRN_WIDTH + GDN_QKV_WIDTH + GDN_WIDTH + GDN_V_HEADS,
)

kernel_name = "hymba_hgrn2_gdn_convffn"


def rms_norm(x, w):
    xf = x.astype(jnp.float32)
    y = xf * lax.rsqrt(jnp.mean(xf * xf, axis=-1, keepdims=True) + EPS)
    return (y * w.astype(jnp.float32)).astype(x.dtype)


def l2norm(x):
    return x * lax.rsqrt(jnp.sum(x * x, axis=-1, keepdims=True) + EPS)


def causal_dwconv(x, w):
    W = w.shape[0]
    S = x.shape[1]
    xp = jnp.pad(x, ((0, 0), (W - 1, 0), (0, 0)))
    y = xp[:, W - 1:W - 1 + S, :] * w[W - 1]
    for j in range(W - 1):
        y = y + xp[:, j:j + S, :] * w[j]
    return y


def _chunk(x):
    B, S, H, D = x.shape
    return x.reshape(B, S // CHUNK, CHUNK, H, D).transpose(1, 0, 3, 2, 4)


def _chunk_scalar(x):
    B, S, H = x.shape
    return x.reshape(B, S // CHUNK, CHUNK, H).transpose(1, 0, 3, 2)


def _unchunk(x):
    nc, B, H, C, D = x.shape
    return x.transpose(1, 0, 3, 2, 4).reshape(B, nc * C, H, D)


def hgrn2_chunked(q, k, v, log_f):
    qc, kc, vc, fc = _chunk(q), _chunk(k), _chunk(v), _chunk(log_f)
    bcum = jnp.cumsum(fc, axis=-2)
    causal = jnp.tril(jnp.ones((CHUNK, CHUNK), dtype=bool))

    def step(S, inp):
        q_, k_, v_, b_ = inp
        diff = b_[:, :, :, None, :] - b_[:, :, None, :, :]
        decay = jnp.exp(jnp.where(causal[:, :, None], diff, -jnp.inf))
        attn = jnp.einsum('bhtc,bhsc,bhtsc->bhts', q_, k_, decay)
        b_last = b_[:, :, -1:, :]
        o = (jnp.einsum('bhtc,bhcv->bhtv', q_ * jnp.exp(b_), S)
             + jnp.einsum('bhts,bhsv->bhtv', attn, v_))
        S = (S * jnp.exp(b_last[:, :, 0, :, None])
             + jnp.einsum('bhsc,bhsv->bhcv', k_ * jnp.exp(b_last - b_), v_))
        return S, o

    B, _, H, dk = q.shape
    dv = v.shape[-1]
    S0 = jnp.zeros((B, H, dk, dv), jnp.float32)
    _, o = lax.scan(step, S0, (qc, kc, vc, bcum))
    return _unchunk(o)


def gated_delta_chunked(q, k, v, g, beta):
    qc, kc, vc = _chunk(q), _chunk(k), _chunk(v)
    gch, bch = _chunk_scalar(g), _chunk_scalar(beta)
    gc = jnp.cumsum(gch, axis=-1)
    incl = jnp.tril(jnp.ones((CHUNK, CHUNK), dtype=bool))
    strict = jnp.tril(jnp.ones((CHUNK, CHUNK), dtype=bool), -1)
    L = jnp.exp(jnp.where(incl, gc[..., :, None] - gc[..., None, :], -jnp.inf))
    k_beta = kc * bch[..., None]
    kkt = jnp.einsum('nbhtk,nbhsk->nbhts', k_beta, kc) * L
    M = jnp.eye(CHUNK, dtype=jnp.float32) + jnp.where(strict, kkt, 0.0)
    dv = v.shape[-1]
    rhs = jnp.concatenate([vc * bch[..., None], k_beta * jnp.exp(gc)[..., None]], axis=-1)
    sol = lax.linalg.triangular_solve(M, rhs, left_side=True, lower=True, unit_diagonal=True)
    u, w = sol[..., :dv], sol[..., dv:]
    qk = jnp.einsum('nbhtk,nbhsk->nbhts', qc, kc) * L
    q_dec = qc * jnp.exp(gc)[..., None]
    k_dec = kc * jnp.exp(gc[..., -1:] - gc)[..., None]
    g_last = jnp.exp(gc[..., -1])

    def step(S, inp):
        u_, w_, qk_, qd_, kd_, gl_ = inp
        v_new = u_ - jnp.einsum('bhtk,bhkv->bhtv', w_, S)
        o = (jnp.einsum('bhtk,bhkv->bhtv', qd_, S)
             + jnp.einsum('bhts,bhsv->bhtv', qk_, v_new))
        S = S * gl_[..., None, None] + jnp.einsum('bhsk,bhsv->bhkv', kd_, v_new)
        return S, o

    B, _, H, dk = q.shape
    S0 = jnp.zeros((B, H, dk, dv), jnp.float32)
    _, o = lax.scan(step, S0, (u, w, qk, q_dec, k_dec, g_last))
    return _unchunk(o)


def hybrid_mixer(x, nw, w_in, lb, hgrn_nw, conv_w, A_log, dt_bias, gdn_nw, w_out):
    B, S, _ = x.shape
    f32 = jnp.float32
    h = rms_norm(x, nw)
    proj = h @ w_in
    hq, hf, hi, hg, qkv, z, b_raw, a_raw = jnp.split(proj, SPLIT_OFFSETS, axis=-1)

    hshape = (B, S, HGRN_HEADS, HEAD_DIM)
    q_h = (jax.nn.silu(hq.astype(f32)) * (HEAD_DIM ** -0.5)).reshape(hshape)
    f = lb + (1.0 - lb) * jax.nn.sigmoid(hf.astype(f32))
    log_f = jnp.log(f).reshape(hshape)
    k_h = (1.0 - f).reshape(hshape)
    v_h = hi.astype(f32).reshape(hshape)
    o_h = hgrn2_chunked(q_h, k_h, v_h, log_f)
    o_h = rms_norm(o_h, hgrn_nw) * jax.nn.silu(hg.astype(f32).reshape(hshape))

    qkv = jax.nn.silu(causal_dwconv(qkv, conv_w)).astype(f32)
    q_g, k_g, v_g = jnp.split(qkv, (GDN_K_WIDTH, 2 * GDN_K_WIDTH), axis=-1)
    rep = GDN_V_HEADS // GDN_K_HEADS
    q_g = l2norm(q_g.reshape(B, S, GDN_K_HEADS, HEAD_DIM)) * (HEAD_DIM ** -0.5)
    k_g = l2norm(k_g.reshape(B, S, GDN_K_HEADS, HEAD_DIM))
    q_g = jnp.repeat(q_g, rep, axis=2)
    k_g = jnp.repeat(k_g, rep, axis=2)
    v_g = v_g.reshape(B, S, GDN_V_HEADS, HEAD_DIM)
    beta = jax.nn.sigmoid(b_raw.astype(f32))
    g = -jnp.exp(A_log.astype(f32)) * jax.nn.softplus(a_raw.astype(f32) + dt_bias.astype(f32))
    o_g = gated_delta_chunked(q_g, k_g, v_g, g, beta)
    o_g = rms_norm(o_g, gdn_nw) * jax.nn.silu(z.astype(f32).reshape(B, S, GDN_V_HEADS, HEAD_DIM))

    o = jnp.concatenate([o_h.reshape(B, S, HGRN_WIDTH), o_g.reshape(B, S, GDN_WIDTH)], axis=-1)
    return o.astype(x.dtype) @ w_out


def conv_ffn(x, nw, w_in, cw, cb, w_out):
    h = rms_norm(x, nw)
    gate, up = jnp.split(h @ w_in, 2, axis=-1)
    gate = causal_dwconv(gate, cw) + cb
    return (jax.nn.silu(gate) * up) @ w_out


def setup_inputs(seed: int = 0) -> dict:
    key = jax.random.key(seed)
    ks = jax.random.split(key, 16)
    nrm = jax.random.normal
    x = nrm(ks[0], (BATCH, SEQ, D_MODEL), jnp.float32)
    norm_mix_w = 1.0 + 0.02 * nrm(ks[1], (DEPTH, D_MODEL), jnp.float32)
    w_in = nrm(ks[2], (DEPTH, D_MODEL, PROJ_WIDTH), jnp.float32) * D_MODEL ** -0.5
    hgrn_lb_raw = 0.5 * nrm(ks[3], (DEPTH + 1, HGRN_WIDTH), jnp.float32)
    hgrn_norm_w = 1.0 + 0.02 * nrm(ks[4], (DEPTH, HEAD_DIM), jnp.float32)
    gdn_conv_w = nrm(ks[5], (DEPTH, GDN_CONV_WIDTH, GDN_QKV_WIDTH), jnp.float32) * GDN_CONV_WIDTH ** -0.5
    gdn_A_log = jnp.log(jax.random.uniform(ks[6], (DEPTH, GDN_V_HEADS), jnp.float32, 1.0, 16.0))
    dt = jnp.exp(jax.random.uniform(ks[7], (DEPTH, GDN_V_HEADS), jnp.float32,
                                    math.log(1e-3), math.log(1e-1)))
    gdn_dt_bias = dt + jnp.log(-jnp.expm1(-dt))
    gdn_norm_w = 1.0 + 0.02 * nrm(ks[8], (DEPTH, HEAD_DIM), jnp.float32)
    w_out = nrm(ks[9], (DEPTH, MIX_WIDTH, D_MODEL), jnp.float32) * MIX_WIDTH ** -0.5
    norm_ffn_w = 1.0 + 0.02 * nrm(ks[10], (DEPTH, D_MODEL), jnp.float32)
    w_ffn_in = nrm(ks[11], (DEPTH, D_MODEL, 2 * D_FF), jnp.float32) * D_MODEL ** -0.5
    ffn_conv_w = nrm(ks[12], (DEPTH, FFN_CONV_WIDTH, D_FF), jnp.float32) * FFN_CONV_WIDTH ** -0.5
    ffn_conv_b = 0.02 * nrm(ks[13], (DEPTH, D_FF), jnp.float32)
    w_ffn_out = nrm(ks[14], (DEPTH, D_FF, D_MODEL), jnp.float32) * D_FF ** -0.5
    final_norm_w = 1.0 + 0.02 * nrm(ks[15], (D_MODEL,), jnp.float32)
    return {"x": x, "norm_mix_w": norm_mix_w, "w_in": w_in, "hgrn_lb_raw": hgrn_lb_raw,
            "hgrn_norm_w": hgrn_norm_w, "gdn_conv_w": gdn_conv_w, "gdn_A_log": gdn_A_log,
            "gdn_dt_bias": gdn_dt_bias, "gdn_norm_w": gdn_norm_w, "w_out": w_out,
            "norm_ffn_w": norm_ffn_w, "w_ffn_in": w_ffn_in, "ffn_conv_w": ffn_conv_w,
            "ffn_conv_b": ffn_conv_b, "w_ffn_out": w_ffn_out, "final_norm_w": final_norm_w}


def reference(x, norm_mix_w, w_in, hgrn_lb_raw, hgrn_norm_w, gdn_conv_w, gdn_A_log, gdn_dt_bias,
              gdn_norm_w, w_out, norm_ffn_w, w_ffn_in, ffn_conv_w, ffn_conv_b, w_ffn_out,
              final_norm_w):
    lb_all = jnp.cumsum(jax.nn.softmax(hgrn_lb_raw.astype(jnp.float32), axis=0), axis=0)
    for l in range(DEPTH):
        x = x + hybrid_mixer(x, norm_mix_w[l], w_in[l], lb_all[l], hgrn_norm_w[l], gdn_conv_w[l],
                             gdn_A_log[l], gdn_dt_bias[l], gdn_norm_w[l], w_out[l])
        x = x + conv_ffn(x, norm_ffn_w[l], w_ffn_in[l], ffn_conv_w[l], ffn_conv_b[l], w_ffn_out[l])
    return rms_norm(x, final_norm_w)
```

```python
import functools

import jax
import jax.numpy as jnp
from jax import lax
from jax.experimental import pallas as pl
from jax.experimental.pallas import tpu as pltpu

HEAD_DIM = 128
CHUNK = 64
SUB = 16
GDN_CONV_WIDTH = 4
FFN_CONV_WIDTH = 3
EPS = 1e-6
LANES = 128
SUBLANES = 8
VMEM_LIMIT = 56 * 1024 * 1024

_BF = jnp.bfloat16
_F32 = jnp.float32


def _dot(a, b):
    return jnp.dot(a.astype(_BF), b.astype(_BF), preferred_element_type=_F32)


def _dot_nt(a, b):
    return lax.dot_general(a.astype(_BF), b.astype(_BF), (((1,), (1,)), ((), ())),
                           preferred_element_type=_F32)


def _dot_tn(a, b):
    return lax.dot_general(a.astype(_BF), b.astype(_BF), (((0,), (0,)), ((), ())),
                           preferred_element_type=_F32)


def _cumsum_rows(x, tri_bf):
    hi = x.astype(_BF)
    r1 = x - hi.astype(_F32)
    mid = r1.astype(_BF)
    lo = (r1 - mid.astype(_F32)).astype(_BF)
    acc = jnp.dot(tri_bf, hi, preferred_element_type=_F32)
    acc = acc + jnp.dot(tri_bf, mid, preferred_element_type=_F32)
    return acc + jnp.dot(tri_bf, lo, preferred_element_type=_F32)


def _tril_ones(n):
    r = lax.broadcasted_iota(jnp.int32, (n, n), 0)
    c = lax.broadcasted_iota(jnp.int32, (n, n), 1)
    return (r >= c).astype(_BF)


def _silu(x):
    return x * jax.nn.sigmoid(x)


def _rmsnorm_kernel(x_ref, w_ref, o_ref):
    x = x_ref[...]
    ms = jnp.mean(x * x, axis=-1, keepdims=True)
    o_ref[...] = (x * lax.rsqrt(ms + EPS) * w_ref[...]).astype(o_ref.dtype)


def _rmsnorm(x, w, out_dtype, tm=256):
    t, d = x.shape
    return pl.pallas_call(
        _rmsnorm_kernel,
        grid=(t // tm,),
        in_specs=[pl.BlockSpec((tm, d), lambda i: (i, 0)), pl.BlockSpec((1, d), lambda i: (0, 0))],
        out_specs=pl.BlockSpec((tm, d), lambda i: (i, 0)),
        out_shape=jax.ShapeDtypeStruct((t, d), out_dtype),
        compiler_params=pltpu.CompilerParams(dimension_semantics=("parallel",),
                                             vmem_limit_bytes=VMEM_LIMIT),
        name="rmsnorm",
    )(x, w.reshape(1, d))


def _mm_kernel(a_ref, b_ref, o_ref):
    o_ref[...] = jnp.dot(a_ref[...], b_ref[...], preferred_element_type=_F32).astype(o_ref.dtype)


def _mm_res_kernel(a_ref, b_ref, r_ref, o_ref):
    acc = jnp.dot(a_ref[...], b_ref[...], preferred_element_type=_F32)
    o_ref[...] = (acc + r_ref[...]).astype(o_ref.dtype)


def _matmul(a, b, *, tm, tn, out_dtype, res=None, name="matmul"):
    m, k = a.shape
    _, n = b.shape
    tm = min(tm, m)
    in_specs = [pl.BlockSpec((tm, k), lambda i, j: (i, 0)), pl.BlockSpec((k, tn), lambda i, j: (0, j))]
    args = [a, b]
    kern = _mm_kernel
    if res is not None:
        in_specs.append(pl.BlockSpec((tm, tn), lambda i, j: (i, j)))
        args.append(res)
        kern = _mm_res_kernel
    return pl.pallas_call(
        kern,
        grid=(m // tm, n // tn),
        in_specs=in_specs,
        out_specs=pl.BlockSpec((tm, tn), lambda i, j: (i, j)),
        out_shape=jax.ShapeDtypeStruct((m, n), out_dtype),
        compiler_params=pltpu.CompilerParams(dimension_semantics=("parallel", "arbitrary"),
                                             vmem_limit_bytes=VMEM_LIMIT),
        name=name,
    )(*args)


def _hgrn_kernel(hq_ref, hf_ref, hi_ref, hg_ref, lb_ref, nw_ref, o_ref,
                 st_ref, q_s, k_s, b_s, *, heads):
    c = pl.program_id(1)

    @pl.when(c == 0)
    def _():
        st_ref[...] = jnp.zeros_like(st_ref)

    lb = lb_ref[...]
    f = lb + (1.0 - lb) * jax.nn.sigmoid(hf_ref[...])
    b_all = _cumsum_rows(jnp.log(f), _tril_ones(CHUNK))
    q_all = _silu(hq_ref[...]) * (HEAD_DIM ** -0.5)
    k_all = 1.0 - f
    for h in range(heads):
        sl = slice(h * HEAD_DIM, (h + 1) * HEAD_DIM)
        q_s[h] = q_all[:, sl]
        k_s[h] = k_all[:, sl]
        b_s[h] = b_all[:, sl]

    nw = nw_ref[...]
    lane = lax.broadcasted_iota(jnp.int32, (SUB, CHUNK), 1)
    row = lax.broadcasted_iota(jnp.int32, (SUB, CHUNK), 0)

    for h in range(heads):
        sl = slice(h * HEAD_DIM, (h + 1) * HEAD_DIM)
        q = q_s[h]
        k = k_s[h]
        b = b_s[h]
        v = hi_ref[:, sl]
        st = st_ref[h]
        b_last = b[CHUNK - 1:CHUNK, :]

        rows = []
        for blk in range(CHUNK // SUB):
            lo = blk * SUB
            qb = q[lo:lo + SUB, :]
            bb = b[lo:lo + SUB, :]
            acc = jnp.zeros((SUB, CHUNK), _F32)
            for s in range(SUB):
                e = jnp.exp(jnp.minimum(bb - bb[s:s + 1, :], 0.0))
                p = qb * e * k[lo + s:lo + s + 1, :]
                acc = jnp.where(lane == lo + s, jnp.sum(p, axis=-1, keepdims=True), acc)
            acc = jnp.where(row + lo >= lane, acc, 0.0)
            if blk > 0:
                b_ref0 = b[lo - 1:lo, :]
                qs = qb * jnp.exp(bb - b_ref0)
                ks = k * jnp.exp(jnp.minimum(b_ref0 - b, 0.0))
                off = _dot_nt(qs, ks)
                acc = jnp.where(lane < lo, off, acc)
            rows.append(acc)
        attn = jnp.concatenate(rows, axis=0)

        o = _dot_nt(q * jnp.exp(b), st) + _dot(attn, v)
        kd = k * jnp.exp(b_last - b)
        st_ref[h] = st * jnp.exp(b_last) + _dot_tn(v, kd)

        ms = jnp.mean(o * o, axis=-1, keepdims=True)
        y = o * lax.rsqrt(ms + EPS) * nw * _silu(hg_ref[:, sl])
        o_ref[:, sl] = y.astype(o_ref.dtype)


def _hgrn(proj, lb, nw, *, batch, seq, col0, out_dtype=_BF):
    width = lb.shape[-1]
    heads = width // HEAD_DIM
    nc = seq // CHUNK
    cb = col0 // width

    def spec(off):
        return pl.BlockSpec((CHUNK, width), lambda bi, ci, off=off: (bi * nc + ci, cb + off))

    return pl.pallas_call(
        functools.partial(_hgrn_kernel, heads=heads),
        grid=(batch, nc),
        in_specs=[spec(0), spec(1), spec(2), spec(3),
                  pl.BlockSpec((1, width), lambda bi, ci: (0, 0)),
                  pl.BlockSpec((1, HEAD_DIM), lambda bi, ci: (0, 0))],
        out_specs=pl.BlockSpec((CHUNK, width), lambda bi, ci: (bi * nc + ci, 0)),
        out_shape=jax.ShapeDtypeStruct((batch * seq, width), out_dtype),
        scratch_shapes=[pltpu.VMEM((heads, HEAD_DIM, HEAD_DIM), _F32),
                        pltpu.VMEM((heads, CHUNK, HEAD_DIM), _F32),
                        pltpu.VMEM((heads, CHUNK, HEAD_DIM), _F32),
                        pltpu.VMEM((heads, CHUNK, HEAD_DIM), _F32)],
        compiler_params=pltpu.CompilerParams(dimension_semantics=("parallel", "arbitrary"),
                                             vmem_limit_bytes=VMEM_LIMIT),
        name="hgrn2",
    )(proj, proj, proj, proj, lb.reshape(1, width), nw.reshape(1, HEAD_DIM))


def _doubling_masks(r, c):
    eye = (r == c).astype(_F32)
    pair = (r >> 1) == (c >> 1)
    levels = []
    k = 1
    while (2 << k) <= CHUNK:
        levels.append(((r >> (k + 1)) == (c >> (k + 1))) & ((r >> k) != (c >> k)))
        k += 1
    return eye, pair, levels


def _unit_lower_inverse(a, masks):
    eye, pair, levels = masks
    t = eye - jnp.where(pair, a, 0.0)
    for m in levels:
        t = t - _dot(t, _dot(jnp.where(m, a, 0.0), t))
    return t


def _gdn_kernel(qkv_ref, z_ref, ba_ref, cw_ref, alog_ref, dtb_ref, nw_ref, o_ref,
                st_ref, stage_ref, *, k_heads, v_heads):
    c = pl.program_id(1)
    kw = k_heads * HEAD_DIM
    rep = v_heads // k_heads

    @pl.when(c == 0)
    def _():
        st_ref[...] = jnp.zeros_like(st_ref)
        stage_ref[0:SUBLANES, :] = jnp.zeros((SUBLANES, stage_ref.shape[1]), _F32)

    x = qkv_ref[...]
    stage_ref[SUBLANES:SUBLANES + CHUNK, :] = x
    cw = cw_ref[...]
    y = x * cw[GDN_CONV_WIDTH - 1:GDN_CONV_WIDTH, :]
    for j in range(GDN_CONV_WIDTH - 1):
        sh = GDN_CONV_WIDTH - 1 - j
        y = y + stage_ref[SUBLANES - sh:SUBLANES - sh + CHUNK, :] * cw[j:j + 1, :]
    stage_ref[0:SUBLANES, :] = x[CHUNK - SUBLANES:CHUNK, :]
    act = _silu(y)

    ba = ba_ref[...]
    beta_all = jax.nn.sigmoid(ba)
    xs = ba + dtb_ref[...]
    softplus = jnp.maximum(xs, 0.0) + jnp.log(1.0 + jnp.exp(-jnp.abs(xs)))
    g_all = -jnp.exp(alog_ref[...]) * softplus
    gc_all = _cumsum_rows(g_all, _tril_ones(CHUNK))
    beta_t = beta_all.T
    gc_t = gc_all.T

    r = lax.broadcasted_iota(jnp.int32, (CHUNK, CHUNK), 0)
    cc = lax.broadcasted_iota(jnp.int32, (CHUNK, CHUNK), 1)
    nw = nw_ref[...]
    masks = _doubling_masks(r, cc)

    for kh in range(k_heads):
        qh = act[:, kh * HEAD_DIM:(kh + 1) * HEAD_DIM]
        kk_ = act[:, kw + kh * HEAD_DIM:kw + (kh + 1) * HEAD_DIM]
        qh = qh * lax.rsqrt(jnp.sum(qh * qh, axis=-1, keepdims=True) + EPS) * (HEAD_DIM ** -0.5)
        kn = kk_ * lax.rsqrt(jnp.sum(kk_ * kk_, axis=-1, keepdims=True) + EPS)
        kkt = _dot_nt(kn, kn)
        qkt = _dot_nt(qh, kn)
        for rr in range(rep):
            h = kh * rep + rr
            vh = act[:, 2 * kw + h * HEAD_DIM:2 * kw + (h + 1) * HEAD_DIM]
            bcol = beta_all[:, h:h + 1]
            brow = beta_t[h:h + 1, :]
            gcol = gc_all[:, v_heads + h:v_heads + h + 1]
            grow = gc_t[v_heads + h:v_heads + h + 1, :]
            glast = gcol[CHUNK - 1:CHUNK, :]
            decay = jnp.where(r >= cc, jnp.exp(jnp.minimum(gcol - grow, 0.0)), 0.0)
            a = jnp.where(r > cc, kkt * (bcol * decay), 0.0)
            t = _unit_lower_inverse(a, masks)
            tb = t * brow
            st = st_ref[h]
            u = _dot(tb, vh)
            w = _dot(tb * jnp.exp(grow), kn)
            v_new = u - _dot(w, st)
            o = jnp.exp(gcol) * _dot(qh, st) + _dot(qkt * decay, v_new)
            st_ref[h] = st * jnp.exp(glast) + _dot_tn(kn, v_new * jnp.exp(glast - gcol))

            ms = jnp.mean(o * o, axis=-1, keepdims=True)
            zh = z_ref[:, h * HEAD_DIM:(h + 1) * HEAD_DIM]
            yh = o * lax.rsqrt(ms + EPS) * nw * _silu(zh)
            o_ref[:, h * HEAD_DIM:(h + 1) * HEAD_DIM] = yh.astype(o_ref.dtype)


def _gdn(proj, ba, conv_w, alog_pad, dtb_pad, nw, *, batch, seq, qkv_col0, z_col0, k_heads, v_heads,
         out_dtype=_BF):
    qkv_w = (2 * k_heads + v_heads) * HEAD_DIM
    vw = v_heads * HEAD_DIM
    nc = seq // CHUNK
    qb = qkv_col0 // qkv_w
    zb = z_col0 // vw
    return pl.pallas_call(
        functools.partial(_gdn_kernel, k_heads=k_heads, v_heads=v_heads),
        grid=(batch, nc),
        in_specs=[pl.BlockSpec((CHUNK, qkv_w), lambda bi, ci: (bi * nc + ci, qb)),
                  pl.BlockSpec((CHUNK, vw), lambda bi, ci: (bi * nc + ci, zb)),
                  pl.BlockSpec((CHUNK, LANES), lambda bi, ci: (bi * nc + ci, 0)),
                  pl.BlockSpec((GDN_CONV_WIDTH, qkv_w), lambda bi, ci: (0, 0)),
                  pl.BlockSpec((1, LANES), lambda bi, ci: (0, 0)),
                  pl.BlockSpec((1, LANES), lambda bi, ci: (0, 0)),
                  pl.BlockSpec((1, HEAD_DIM), lambda bi, ci: (0, 0))],
        out_specs=pl.BlockSpec((CHUNK, vw), lambda bi, ci: (bi * nc + ci, 0)),
        out_shape=jax.ShapeDtypeStruct((batch * seq, vw), out_dtype),
        scratch_shapes=[pltpu.VMEM((v_heads, HEAD_DIM, HEAD_DIM), _F32),
                        pltpu.VMEM((SUBLANES + CHUNK, qkv_w), _F32)],
        compiler_params=pltpu.CompilerParams(dimension_semantics=("parallel", "arbitrary"),
                                             vmem_limit_bytes=VMEM_LIMIT),
        name="gated_delta",
    )(proj, proj, ba, conv_w, alog_pad, dtb_pad, nw.reshape(1, HEAD_DIM))


def _mm2_res_kernel(a1_ref, a2_ref, b_ref, r_ref, o_ref):
    k1 = a1_ref.shape[1]
    acc = jnp.dot(a1_ref[...], b_ref[0:k1, :], preferred_element_type=_F32)
    acc = acc + jnp.dot(a2_ref[...], b_ref[k1:, :], preferred_element_type=_F32)
    o_ref[...] = (acc + r_ref[...]).astype(o_ref.dtype)


def _out_proj(a1, a2, b, res, *, tm, tn):
    m, k1 = a1.shape
    _, k2 = a2.shape
    _, n = b.shape
    tm = min(tm, m)
    return pl.pallas_call(
        _mm2_res_kernel,
        grid=(m // tm, n // tn),
        in_specs=[pl.BlockSpec((tm, k1), lambda i, j: (i, 0)),
                  pl.BlockSpec((tm, k2), lambda i, j: (i, 0)),
                  pl.BlockSpec((k1 + k2, tn), lambda i, j: (0, j)),
                  pl.BlockSpec((tm, tn), lambda i, j: (i, j))],
        out_specs=pl.BlockSpec((tm, tn), lambda i, j: (i, j)),
        out_shape=jax.ShapeDtypeStruct((m, n), _F32),
        compiler_params=pltpu.CompilerParams(dimension_semantics=("parallel", "arbitrary"),
                                             vmem_limit_bytes=VMEM_LIMIT),
        name="out_proj",
    )(a1, a2, b, res)


def _ffn_act_kernel(g_ref, u_ref, cw_ref, cb_ref, o_ref, stage_ref):
    c = pl.program_id(2)
    rows = g_ref.shape[0]

    @pl.when(c == 0)
    def _():
        stage_ref[0:SUBLANES, :] = jnp.zeros((SUBLANES, stage_ref.shape[1]), _F32)

    g = g_ref[...]
    stage_ref[SUBLANES:SUBLANES + rows, :] = g
    cw = cw_ref[...]
    y = g * cw[FFN_CONV_WIDTH - 1:FFN_CONV_WIDTH, :] + cb_ref[...]
    for j in range(FFN_CONV_WIDTH - 1):
        sh = FFN_CONV_WIDTH - 1 - j
        y = y + stage_ref[SUBLANES - sh:SUBLANES - sh + rows, :] * cw[j:j + 1, :]
    stage_ref[0:SUBLANES, :] = g[rows - SUBLANES:rows, :]
    o_ref[...] = (_silu(y) * u_ref[...]).astype(o_ref.dtype)


def _ffn_act(gu, cw, cb, *, batch, seq, d_ff, tr, tc):
    tr = min(tr, seq)
    nr = seq // tr
    ncol = d_ff // tc
    return pl.pallas_call(
        _ffn_act_kernel,
        grid=(batch, ncol, nr),
        in_specs=[pl.BlockSpec((tr, tc), lambda bi, ji, ri: (bi * nr + ri, ji)),
                  pl.BlockSpec((tr, tc), lambda bi, ji, ri: (bi * nr + ri, ncol + ji)),
                  pl.BlockSpec((FFN_CONV_WIDTH, tc), lambda bi, ji, ri: (0, ji)),
                  pl.BlockSpec((1, tc), lambda bi, ji, ri: (0, ji))],
        out_specs=pl.BlockSpec((tr, tc), lambda bi, ji, ri: (bi * nr + ri, ji)),
        out_shape=jax.ShapeDtypeStruct((batch * seq, d_ff), _BF),
        scratch_shapes=[pltpu.VMEM((SUBLANES + tr, tc), _F32)],
        compiler_params=pltpu.CompilerParams(
            dimension_semantics=("parallel", "parallel", "arbitrary"), vmem_limit_bytes=VMEM_LIMIT),
        name="ffn_act",
    )(gu, gu, cw, cb.reshape(1, d_ff))


def kernel(x, norm_mix_w, w_in, hgrn_lb_raw, hgrn_norm_w, gdn_conv_w, gdn_A_log, gdn_dt_bias, gdn_norm_w,
           w_out, norm_ffn_w, w_ffn_in, ffn_conv_w, ffn_conv_b, w_ffn_out, final_norm_w):
    batch, seq, d = x.shape
    depth = w_in.shape[0]
    v_heads = gdn_A_log.shape[1]
    gdn_w = v_heads * HEAD_DIM
    k_heads = (gdn_conv_w.shape[2] - gdn_w) // (2 * HEAD_DIM)
    qkv_w = gdn_conv_w.shape[2]
    hgrn_w = hgrn_lb_raw.shape[1]
    d_ff = ffn_conv_b.shape[1]
    main_w = 4 * hgrn_w + qkv_w + gdn_w
    t = batch * seq

    lb_all = jnp.cumsum(jax.nn.softmax(hgrn_lb_raw.astype(_F32), axis=0), axis=0)

    xt = x.reshape(t, d)
    for l in range(depth):
        w_in_l = w_in[l]
        w_main = w_in_l[:, :main_w].astype(_BF)
        w_gate = jnp.pad(w_in_l[:, main_w:], ((0, 0), (0, LANES - 2 * v_heads))).astype(_BF)
        pad_lo = jnp.zeros((v_heads,), _F32)
        pad_hi = jnp.zeros((LANES - 2 * v_heads,), _F32)
        alog_pad = jnp.concatenate([pad_lo, gdn_A_log[l].astype(_F32), pad_hi]).reshape(1, LANES)
        dtb_pad = jnp.concatenate([pad_lo, gdn_dt_bias[l].astype(_F32), pad_hi]).reshape(1, LANES)

        h = _rmsnorm(xt, norm_mix_w[l], _BF)
        proj = _matmul(h, w_main, tm=1024, tn=1024, out_dtype=_F32, name="in_proj")
        ba = _matmul(h, w_gate, tm=1024, tn=LANES, out_dtype=_F32, name="in_proj_gates")

        o_h = _hgrn(proj, lb_all[l], hgrn_norm_w[l], batch=batch, seq=seq, col0=0)
        o_g = _gdn(proj, ba, gdn_conv_w[l], alog_pad, dtb_pad, gdn_norm_w[l], batch=batch, seq=seq,
                   qkv_col0=4 * hgrn_w, z_col0=4 * hgrn_w + qkv_w, k_heads=k_heads, v_heads=v_heads)
        xt = _out_proj(o_h, o_g, w_out[l].astype(_BF), xt, tm=1024, tn=1024)

        h2 = _rmsnorm(xt, norm_ffn_w[l], _BF)
        gu = _matmul(h2, w_ffn_in[l].astype(_BF), tm=1024, tn=512, out_dtype=_F32, name="ffn_in")
        act = _ffn_act(gu, ffn_conv_w[l], ffn_conv_b[l], batch=batch, seq=seq, d_ff=d_ff, tr=512, tc=256)
        xt = _matmul(act, w_ffn_out[l].astype(_BF), tm=512, tn=512, out_dtype=_F32, res=xt, name="ffn_out")

    out = _rmsnorm(xt, final_norm_w, x.dtype)
    return out.reshape(batch, seq, d)
```

```python
import functools

import jax
import jax.numpy as jnp
from jax import lax
from jax.experimental import pallas as pl
from jax.experimental.pallas import tpu as pltpu

HEAD_DIM = 128
CHUNK = 64
SUB = 16
GDN_CONV_WIDTH = 4
FFN_CONV_WIDTH = 3
EPS = 1e-6
LANES = 128
SUBLANES = 8
VMEM_LIMIT = 56 * 1024 * 1024

_BF = jnp.bfloat16
_F32 = jnp.float32


def _dot(a, b):
    return jnp.dot(a.astype(_BF), b.astype(_BF), preferred_element_type=_F32)


def _dot_nt(a, b):
    return lax.dot_general(a.astype(_BF), b.astype(_BF), (((1,), (1,)), ((), ())),
                           preferred_element_type=_F32)


def _dot_tn(a, b):
    return lax.dot_general(a.astype(_BF), b.astype(_BF), (((0,), (0,)), ((), ())),
                           preferred_element_type=_F32)


def _cumsum_rows(x, tri_bf):
    hi = x.astype(_BF)
    r1 = x - hi.astype(_F32)
    mid = r1.astype(_BF)
    lo = (r1 - mid.astype(_F32)).astype(_BF)
    acc = jnp.dot(tri_bf, hi, preferred_element_type=_F32)
    acc = acc + jnp.dot(tri_bf, mid, preferred_element_type=_F32)
    return acc + jnp.dot(tri_bf, lo, preferred_element_type=_F32)


def _tril_ones(n):
    r = lax.broadcasted_iota(jnp.int32, (n, n), 0)
    c = lax.broadcasted_iota(jnp.int32, (n, n), 1)
    return (r >= c).astype(_BF)


def _silu(x):
    return x * jax.nn.sigmoid(x)


def _rmsnorm_kernel(x_ref, w_ref, o_ref):
    x = x_ref[...]
    ms = jnp.mean(x * x, axis=-1, keepdims=True)
    o_ref[...] = (x * lax.rsqrt(ms + EPS) * w_ref[...]).astype(o_ref.dtype)


def _rmsnorm(x, w, out_dtype, tm=256):
    t, d = x.shape
    return pl.pallas_call(
        _rmsnorm_kernel,
        grid=(t // tm,),
        in_specs=[pl.BlockSpec((tm, d), lambda i: (i, 0)), pl.BlockSpec((1, d), lambda i: (0, 0))],
        out_specs=pl.BlockSpec((tm, d), lambda i: (i, 0)),
        out_shape=jax.ShapeDtypeStruct((t, d), out_dtype),
        compiler_params=pltpu.CompilerParams(dimension_semantics=("parallel",),
                                             vmem_limit_bytes=VMEM_LIMIT),
        name="rmsnorm",
    )(x, w.reshape(1, d))


def _mm_kernel(a_ref, b_ref, o_ref):
    o_ref[...] = jnp.dot(a_ref[...], b_ref[...], preferred_element_type=_F32).astype(o_ref.dtype)


def _mm_res_kernel(a_ref, b_ref, r_ref, o_ref):
    acc = jnp.dot(a_ref[...], b_ref[...], preferred_element_type=_F32)
    o_ref[...] = (acc + r_ref[...]).astype(o_ref.dtype)


def _matmul(a, b, *, tm, tn, out_dtype, res=None, name="matmul"):
    m, k = a.shape
    _, n = b.shape
    tm = min(tm, m)
    in_specs = [pl.BlockSpec((tm, k), lambda i, j: (i, 0)), pl.BlockSpec((k, tn), lambda i, j: (0, j))]
    args = [a, b]
    kern = _mm_kernel
    if res is not None:
        in_specs.append(pl.BlockSpec((tm, tn), lambda i, j: (i, j)))
        args.append(res)
        kern = _mm_res_kernel
    return pl.pallas_call(
        kern,
        grid=(m // tm, n // tn),
        in_specs=in_specs,
        out_specs=pl.BlockSpec((tm, tn), lambda i, j: (i, j)),
        out_shape=jax.ShapeDtypeStruct((m, n), out_dtype),
        compiler_params=pltpu.CompilerParams(dimension_semantics=("parallel", "arbitrary"),
                                             vmem_limit_bytes=VMEM_LIMIT),
        name=name,
    )(*args)


def _mm_wcast_kernel(a_ref, w_ref, o_ref, wbf_ref):
    @pl.when(pl.program_id(1) == 0)
    def _():
        wbf_ref[...] = w_ref[...].astype(_BF)

    o_ref[...] = jnp.dot(a_ref[...], wbf_ref[...], preferred_element_type=_F32).astype(o_ref.dtype)


def _matmul_wcast(a, w, layer, *, n_cols, tm, tn, out_dtype, name):
    m, k = a.shape
    tm = min(tm, m)
    return pl.pallas_call(
        _mm_wcast_kernel,
        grid=(n_cols // tn, m // tm),
        in_specs=[pl.BlockSpec((tm, k), lambda j, i: (i, 0)),
                  pl.BlockSpec((None, k, tn), lambda j, i: (layer, 0, j))],
        out_specs=pl.BlockSpec((tm, tn), lambda j, i: (i, j)),
        out_shape=jax.ShapeDtypeStruct((m, n_cols), out_dtype),
        scratch_shapes=[pltpu.VMEM((k, tn), _BF)],
        compiler_params=pltpu.CompilerParams(dimension_semantics=("parallel", "arbitrary"),
                                             vmem_limit_bytes=VMEM_LIMIT),
        name=name,
    )(a, w)


def _hgrn_kernel(hq_ref, hf_ref, hi_ref, hg_ref, lb_ref, nw_ref, o_ref,
                 st_ref, q_s, k_s, b_s, *, heads):
    c = pl.program_id(1)

    @pl.when(c == 0)
    def _():
        st_ref[...] = jnp.zeros_like(st_ref)

    lb = lb_ref[...]
    f = lb + (1.0 - lb) * jax.nn.sigmoid(hf_ref[...])
    b_all = _cumsum_rows(jnp.log(f), _tril_ones(CHUNK))
    q_all = _silu(hq_ref[...]) * (HEAD_DIM ** -0.5)
    k_all = 1.0 - f
    for h in range(heads):
        sl = slice(h * HEAD_DIM, (h + 1) * HEAD_DIM)
        q_s[h] = q_all[:, sl]
        k_s[h] = k_all[:, sl]
        b_s[h] = b_all[:, sl]

    nw = nw_ref[...]
    lane = lax.broadcasted_iota(jnp.int32, (SUB, CHUNK), 1)
    row = lax.broadcasted_iota(jnp.int32, (SUB, CHUNK), 0)

    for h in range(heads):
        sl = slice(h * HEAD_DIM, (h + 1) * HEAD_DIM)
        q = q_s[h]
        k = k_s[h]
        b = b_s[h]
        v = hi_ref[:, sl]
        st = st_ref[h]
        b_last = b[CHUNK - 1:CHUNK, :]

        rows = []
        for blk in range(CHUNK // SUB):
            lo = blk * SUB
            qb = q[lo:lo + SUB, :]
            bb = b[lo:lo + SUB, :]
            acc = jnp.zeros((SUB, CHUNK), _F32)
            for s in range(SUB):
                e = jnp.exp(jnp.minimum(bb - bb[s:s + 1, :], 0.0))
                p = qb * e * k[lo + s:lo + s + 1, :]
                acc = jnp.where(lane == lo + s, jnp.sum(p, axis=-1, keepdims=True), acc)
            acc = jnp.where(row + lo >= lane, acc, 0.0)
            if blk > 0:
                b_ref0 = b[lo - 1:lo, :]
                qs = qb * jnp.exp(bb - b_ref0)
                ks = k * jnp.exp(jnp.minimum(b_ref0 - b, 0.0))
                off = _dot_nt(qs, ks)
                acc = jnp.where(lane < lo, off, acc)
            rows.append(acc)
        attn = jnp.concatenate(rows, axis=0)

        o = _dot_nt(q * jnp.exp(b), st) + _dot(attn, v)
        kd = k * jnp.exp(b_last - b)
        st_ref[h] = st * jnp.exp(b_last) + _dot_tn(v, kd)

        ms = jnp.mean(o * o, axis=-1, keepdims=True)
        y = o * lax.rsqrt(ms + EPS) * nw * _silu(hg_ref[:, sl])
        o_ref[:, sl] = y.astype(o_ref.dtype)


def _hgrn(proj, lb, nw, *, batch, seq, col0, out_dtype=_BF):
    width = lb.shape[-1]
    heads = width // HEAD_DIM
    nc = seq // CHUNK
    cb = col0 // width

    def spec(off):
        return pl.BlockSpec((CHUNK, width), lambda bi, ci, off=off: (bi * nc + ci, cb + off))

    return pl.pallas_call(
        functools.partial(_hgrn_kernel, heads=heads),
        grid=(batch, nc),
        in_specs=[spec(0), spec(1), spec(2), spec(3),
                  pl.BlockSpec((1, width), lambda bi, ci: (0, 0)),
                  pl.BlockSpec((1, HEAD_DIM), lambda bi, ci: (0, 0))],
        out_specs=pl.BlockSpec((CHUNK, width), lambda bi, ci: (bi * nc + ci, 0)),
        out_shape=jax.ShapeDtypeStruct((batch * seq, width), out_dtype),
        scratch_shapes=[pltpu.VMEM((heads, HEAD_DIM, HEAD_DIM), _F32),
                        pltpu.VMEM((heads, CHUNK, HEAD_DIM), _F32),
                        pltpu.VMEM((heads, CHUNK, HEAD_DIM), _F32),
                        pltpu.VMEM((heads, CHUNK, HEAD_DIM), _F32)],
        compiler_params=pltpu.CompilerParams(dimension_semantics=("parallel", "arbitrary"),
                                             vmem_limit_bytes=VMEM_LIMIT),
        name="hgrn2",
    )(proj, proj, proj, proj, lb.reshape(1, width), nw.reshape(1, HEAD_DIM))


def _doubling_masks(r, c):
    eye = (r == c).astype(_F32)
    pair = (r >> 1) == (c >> 1)
    levels = []
    k = 1
    while (2 << k) <= CHUNK:
        levels.append(((r >> (k + 1)) == (c >> (k + 1))) & ((r >> k) != (c >> k)))
        k += 1
    return eye, pair, levels


def _unit_lower_inverses(mats, masks):
    eye, pair, levels = masks
    ts = [eye - jnp.where(pair, a, 0.0) for a in mats]
    for m in levels:
        xs = [_dot(jnp.where(m, a, 0.0), t) for a, t in zip(mats, ts)]
        ts = [t - _dot(t, x) for t, x in zip(ts, xs)]
    return ts


def _gdn_kernel(qkv_ref, z_ref, ba_ref, cw_ref, alog_ref, dtb_ref, nw_ref, o_ref,
                st_ref, stage_ref, *, k_heads, v_heads):
    c = pl.program_id(1)
    kw = k_heads * HEAD_DIM
    rep = v_heads // k_heads

    @pl.when(c == 0)
    def _():
        st_ref[...] = jnp.zeros_like(st_ref)
        stage_ref[0:SUBLANES, :] = jnp.zeros((SUBLANES, stage_ref.shape[1]), _F32)

    x = qkv_ref[...]
    stage_ref[SUBLANES:SUBLANES + CHUNK, :] = x
    cw = cw_ref[...]
    y = x * cw[GDN_CONV_WIDTH - 1:GDN_CONV_WIDTH, :]
    for j in range(GDN_CONV_WIDTH - 1):
        sh = GDN_CONV_WIDTH - 1 - j
        y = y + stage_ref[SUBLANES - sh:SUBLANES - sh + CHUNK, :] * cw[j:j + 1, :]
    stage_ref[0:SUBLANES, :] = x[CHUNK - SUBLANES:CHUNK, :]
    act = _silu(y)

    ba = ba_ref[...]
    beta_all = jax.nn.sigmoid(ba)
    xs = ba + dtb_ref[...]
    softplus = jnp.maximum(xs, 0.0) + jnp.log(1.0 + jnp.exp(-jnp.abs(xs)))
    g_all = -jnp.exp(alog_ref[...]) * softplus
    gc_all = _cumsum_rows(g_all, _tril_ones(CHUNK))
    beta_t = beta_all.T
    gc_t = gc_all.T

    r = lax.broadcasted_iota(jnp.int32, (CHUNK, CHUNK), 0)
    cc = lax.broadcasted_iota(jnp.int32, (CHUNK, CHUNK), 1)
    nw = nw_ref[...]
    masks = _doubling_masks(r, cc)

    hs = range(v_heads)
    qn, kn = [], []
    for kh in range(k_heads):
        qh = act[:, kh * HEAD_DIM:(kh + 1) * HEAD_DIM]
        kk_ = act[:, kw + kh * HEAD_DIM:kw + (kh + 1) * HEAD_DIM]
        qh = qh * lax.rsqrt(jnp.sum(qh * qh, axis=-1, keepdims=True) + EPS) * (HEAD_DIM ** -0.5)
        kk_ = kk_ * lax.rsqrt(jnp.sum(kk_ * kk_, axis=-1, keepdims=True) + EPS)
        qn.append(qh.astype(_BF))
        kn.append(kk_.astype(_BF))
    kkt = [_dot_nt(k_, k_) for k_ in kn]
    qkt = [_dot_nt(q_, k_) for q_, k_ in zip(qn, kn)]
    vh = [act[:, 2 * kw + h * HEAD_DIM:2 * kw + (h + 1) * HEAD_DIM].astype(_BF) for h in hs]

    bcol = [beta_all[:, h:h + 1] for h in hs]
    brow = [beta_t[h:h + 1, :] for h in hs]
    gcol = [gc_all[:, v_heads + h:v_heads + h + 1] for h in hs]
    grow = [gc_t[v_heads + h:v_heads + h + 1, :] for h in hs]
    glast = [g[CHUNK - 1:CHUNK, :] for g in gcol]
    decay = [jnp.where(r >= cc, jnp.exp(jnp.minimum(gcol[h] - grow[h], 0.0)), 0.0) for h in hs]
    amat = [jnp.where(r > cc, kkt[h // rep] * (bcol[h] * decay[h]), 0.0) for h in hs]
    tinv = _unit_lower_inverses(amat, masks)
    tb = [tinv[h] * brow[h] for h in hs]
    u = [_dot(tb[h], vh[h]) for h in hs]
    w = [_dot(tb[h] * jnp.exp(grow[h]), kn[h // rep]) for h in hs]
    qkl = [(qkt[h // rep] * decay[h]).astype(_BF) for h in hs]

    st = [st_ref[h] for h in hs]
    st_bf = [s_.astype(_BF) for s_ in st]
    ws = [_dot(w[h], st_bf[h]) for h in hs]
    qs = [_dot(qn[h // rep], st_bf[h]) for h in hs]
    v_new = [u[h] - ws[h] for h in hs]
    o_intra = [_dot(qkl[h], v_new[h]) for h in hs]
    kv = [_dot_tn(kn[h // rep], v_new[h] * jnp.exp(glast[h] - gcol[h])) for h in hs]
    for h in hs:
        st_ref[h] = st[h] * jnp.exp(glast[h]) + kv[h]
        o = jnp.exp(gcol[h]) * qs[h] + o_intra[h]
        ms = jnp.mean(o * o, axis=-1, keepdims=True)
        zh = z_ref[:, h * HEAD_DIM:(h + 1) * HEAD_DIM]
        yh = o * lax.rsqrt(ms + EPS) * nw * _silu(zh)
        o_ref[:, h * HEAD_DIM:(h + 1) * HEAD_DIM] = yh.astype(o_ref.dtype)


def _gdn(proj, ba, conv_w, alog_pad, dtb_pad, nw, *, batch, seq, qkv_col0, z_col0, k_heads, v_heads,
         out_dtype=_BF):
    qkv_w = (2 * k_heads + v_heads) * HEAD_DIM
    vw = v_heads * HEAD_DIM
    nc = seq // CHUNK
    qb = qkv_col0 // qkv_w
    zb = z_col0 // vw
    return pl.pallas_call(
        functools.partial(_gdn_kernel, k_heads=k_heads, v_heads=v_heads),
        grid=(batch, nc),
        in_specs=[pl.BlockSpec((CHUNK, qkv_w), lambda bi, ci: (bi * nc + ci, qb)),
                  pl.BlockSpec((CHUNK, vw), lambda bi, ci: (bi * nc + ci, zb)),
                  pl.BlockSpec((CHUNK, LANES), lambda bi, ci: (bi * nc + ci, 0)),
                  pl.BlockSpec((GDN_CONV_WIDTH, qkv_w), lambda bi, ci: (0, 0)),
                  pl.BlockSpec((1, LANES), lambda bi, ci: (0, 0)),
                  pl.BlockSpec((1, LANES), lambda bi, ci: (0, 0)),
                  pl.BlockSpec((1, HEAD_DIM), lambda bi, ci: (0, 0))],
        out_specs=pl.BlockSpec((CHUNK, vw), lambda bi, ci: (bi * nc + ci, 0)),
        out_shape=jax.ShapeDtypeStruct((batch * seq, vw), out_dtype),
        scratch_shapes=[pltpu.VMEM((v_heads, HEAD_DIM, HEAD_DIM), _F32),
                        pltpu.VMEM((SUBLANES + CHUNK, qkv_w), _F32)],
        compiler_params=pltpu.CompilerParams(dimension_semantics=("parallel", "arbitrary"),
                                             vmem_limit_bytes=VMEM_LIMIT),
        name="gated_delta",
    )(proj, proj, ba, conv_w, alog_pad, dtb_pad, nw.reshape(1, HEAD_DIM))


def _mm2_res_kernel(a1_ref, a2_ref, w_ref, r_ref, o_ref, wbf_ref):
    @pl.when(pl.program_id(1) == 0)
    def _():
        wbf_ref[...] = w_ref[...].astype(_BF)

    k1 = a1_ref.shape[1]
    acc = jnp.dot(a1_ref[...], wbf_ref[0:k1, :], preferred_element_type=_F32)
    acc = acc + jnp.dot(a2_ref[...], wbf_ref[k1:, :], preferred_element_type=_F32)
    o_ref[...] = (acc + r_ref[...]).astype(o_ref.dtype)


def _out_proj(a1, a2, w, layer, res, *, tm, tn):
    m, k1 = a1.shape
    _, k2 = a2.shape
    n = w.shape[2]
    tm = min(tm, m)
    return pl.pallas_call(
        _mm2_res_kernel,
        grid=(n // tn, m // tm),
        in_specs=[pl.BlockSpec((tm, k1), lambda j, i: (i, 0)),
                  pl.BlockSpec((tm, k2), lambda j, i: (i, 0)),
                  pl.BlockSpec((None, k1 + k2, tn), lambda j, i: (layer, 0, j)),
                  pl.BlockSpec((tm, tn), lambda j, i: (i, j))],
        out_specs=pl.BlockSpec((tm, tn), lambda j, i: (i, j)),
        out_shape=jax.ShapeDtypeStruct((m, n), _F32),
        scratch_shapes=[pltpu.VMEM((k1 + k2, tn), _BF)],
        compiler_params=pltpu.CompilerParams(dimension_semantics=("parallel", "arbitrary"),
                                             vmem_limit_bytes=VMEM_LIMIT),
        name="out_proj",
    )(a1, a2, w, res)


def _ffn_in_kernel(a_ref, wg_ref, wu_ref, cw_ref, cb_ref, o_ref, wg_bf, wu_bf, stage_ref, *, seq_tiles):
    i = pl.program_id(1)
    rows = a_ref.shape[0]

    @pl.when(i == 0)
    def _():
        wg_bf[...] = wg_ref[...].astype(_BF)
        wu_bf[...] = wu_ref[...].astype(_BF)

    @pl.when(i % seq_tiles == 0)
    def _():
        stage_ref[0:SUBLANES, :] = jnp.zeros((SUBLANES, stage_ref.shape[1]), _F32)

    a = a_ref[...]
    g = jnp.dot(a, wg_bf[...], preferred_element_type=_F32)
    u = jnp.dot(a, wu_bf[...], preferred_element_type=_F32)
    stage_ref[SUBLANES:SUBLANES + rows, :] = g
    cw = cw_ref[...]
    y = g * cw[FFN_CONV_WIDTH - 1:FFN_CONV_WIDTH, :] + cb_ref[...]
    for j in range(FFN_CONV_WIDTH - 1):
        sh = FFN_CONV_WIDTH - 1 - j
        y = y + stage_ref[SUBLANES - sh:SUBLANES - sh + rows, :] * cw[j:j + 1, :]
    stage_ref[0:SUBLANES, :] = g[rows - SUBLANES:rows, :]
    o_ref[...] = (_silu(y) * u).astype(o_ref.dtype)


def _ffn_in(a, w, layer, cw, cb, *, seq, tm, tn):
    m, k = a.shape
    d_ff = cb.shape[-1]
    tm = min(tm, seq)
    ncol = d_ff // tn
    return pl.pallas_call(
        functools.partial(_ffn_in_kernel, seq_tiles=seq // tm),
        grid=(ncol, m // tm),
        in_specs=[pl.BlockSpec((tm, k), lambda j, i: (i, 0)),
                  pl.BlockSpec((None, k, tn), lambda j, i: (layer, 0, j)),
                  pl.BlockSpec((None, k, tn), lambda j, i: (layer, 0, ncol + j)),
                  pl.BlockSpec((FFN_CONV_WIDTH, tn), lambda j, i: (0, j)),
                  pl.BlockSpec((1, tn), lambda j, i: (0, j))],
        out_specs=pl.BlockSpec((tm, tn), lambda j, i: (i, j)),
        out_shape=jax.ShapeDtypeStruct((m, d_ff), _BF),
        scratch_shapes=[pltpu.VMEM((k, tn), _BF), pltpu.VMEM((k, tn), _BF),
                        pltpu.VMEM((SUBLANES + tm, tn), _F32)],
        compiler_params=pltpu.CompilerParams(dimension_semantics=("parallel", "arbitrary"),
                                             vmem_limit_bytes=VMEM_LIMIT),
        name="ffn_in",
    )(a, w, w, cw, cb.reshape(1, d_ff))


def kernel(x, norm_mix_w, w_in, hgrn_lb_raw, hgrn_norm_w, gdn_conv_w, gdn_A_log, gdn_dt_bias, gdn_norm_w,
           w_out, norm_ffn_w, w_ffn_in, ffn_conv_w, ffn_conv_b, w_ffn_out, final_norm_w):
    batch, seq, d = x.shape
    depth = w_in.shape[0]
    v_heads = gdn_A_log.shape[1]
    gdn_w = v_heads * HEAD_DIM
    k_heads = (gdn_conv_w.shape[2] - gdn_w) // (2 * HEAD_DIM)
    qkv_w = gdn_conv_w.shape[2]
    hgrn_w = hgrn_lb_raw.shape[1]
    d_ff = ffn_conv_b.shape[1]
    main_w = 4 * hgrn_w + qkv_w + gdn_w
    t = batch * seq

    lb_all = jnp.cumsum(jax.nn.softmax(hgrn_lb_raw.astype(_F32), axis=0), axis=0)

    xt = x.reshape(t, d)
    for l in range(depth):
        w_gate = jnp.pad(w_in[l, :, main_w:], ((0, 0), (0, LANES - 2 * v_heads))).astype(_BF)
        pad_lo = jnp.zeros((v_heads,), _F32)
        pad_hi = jnp.zeros((LANES - 2 * v_heads,), _F32)
        alog_pad = jnp.concatenate([pad_lo, gdn_A_log[l].astype(_F32), pad_hi]).reshape(1, LANES)
        dtb_pad = jnp.concatenate([pad_lo, gdn_dt_bias[l].astype(_F32), pad_hi]).reshape(1, LANES)

        h = _rmsnorm(xt, norm_mix_w[l], _BF)
        proj = _matmul_wcast(h, w_in, l, n_cols=main_w, tm=1024, tn=512, out_dtype=_F32, name="in_proj")
        ba = _matmul(h, w_gate, tm=1024, tn=LANES, out_dtype=_F32, name="in_proj_gates")

        o_h = _hgrn(proj, lb_all[l], hgrn_norm_w[l], batch=batch, seq=seq, col0=0)
        o_g = _gdn(proj, ba, gdn_conv_w[l], alog_pad, dtb_pad, gdn_norm_w[l], batch=batch, seq=seq,
                   qkv_col0=4 * hgrn_w, z_col0=4 * hgrn_w + qkv_w, k_heads=k_heads, v_heads=v_heads)
        xt = _out_proj(o_h, o_g, w_out, l, xt, tm=1024, tn=512)

        h2 = _rmsnorm(xt, norm_ffn_w[l], _BF)
        act = _ffn_in(h2, w_ffn_in, l, ffn_conv_w[l], ffn_conv_b[l], seq=seq, tm=1024, tn=256)
        xt = _matmul(act, w_ffn_out[l].astype(_BF), tm=512, tn=512, out_dtype=_F32, res=xt, name="ffn_out")

    out = _rmsnorm(xt, final_norm_w, x.dtype)
    return out.reshape(batch, seq, d)
```

```python
import functools

import jax
import jax.numpy as jnp
from jax import lax
from jax.experimental import pallas as pl
from jax.experimental.pallas import tpu as pltpu

HEAD_DIM = 128
CHUNK = 64
SUB = 8
LOG2E = 1.4426950408889634
GDN_CONV_WIDTH = 4
FFN_CONV_WIDTH = 3
EPS = 1e-6
LANES = 128
SUBLANES = 8
VMEM_LIMIT = 56 * 1024 * 1024

_BF = jnp.bfloat16
_F32 = jnp.float32


def _dot(a, b):
    return jnp.dot(a.astype(_BF), b.astype(_BF), preferred_element_type=_F32)


def _dot_nt(a, b):
    return lax.dot_general(a.astype(_BF), b.astype(_BF), (((1,), (1,)), ((), ())),
                           preferred_element_type=_F32)


def _dot_tn(a, b):
    return lax.dot_general(a.astype(_BF), b.astype(_BF), (((0,), (0,)), ((), ())),
                           preferred_element_type=_F32)


def _cumsum_rows(x, tri_bf):
    hi = x.astype(_BF)
    r1 = x - hi.astype(_F32)
    mid = r1.astype(_BF)
    lo = (r1 - mid.astype(_F32)).astype(_BF)
    acc = jnp.dot(tri_bf, hi, preferred_element_type=_F32)
    acc = acc + jnp.dot(tri_bf, mid, preferred_element_type=_F32)
    return acc + jnp.dot(tri_bf, lo, preferred_element_type=_F32)


def _tril_ones(n):
    r = lax.broadcasted_iota(jnp.int32, (n, n), 0)
    c = lax.broadcasted_iota(jnp.int32, (n, n), 1)
    return (r >= c).astype(_BF)


def _silu(x):
    return x * jax.nn.sigmoid(x)


def _rmsnorm_kernel(x_ref, w_ref, o_ref):
    x = x_ref[...]
    ms = jnp.mean(x * x, axis=-1, keepdims=True)
    o_ref[...] = (x * lax.rsqrt(ms + EPS) * w_ref[...]).astype(o_ref.dtype)


def _rmsnorm(x, w, out_dtype, tm=256):
    t, d = x.shape
    return pl.pallas_call(
        _rmsnorm_kernel,
        grid=(t // tm,),
        in_specs=[pl.BlockSpec((tm, d), lambda i: (i, 0)), pl.BlockSpec((1, d), lambda i: (0, 0))],
        out_specs=pl.BlockSpec((tm, d), lambda i: (i, 0)),
        out_shape=jax.ShapeDtypeStruct((t, d), out_dtype),
        compiler_params=pltpu.CompilerParams(dimension_semantics=("parallel",),
                                             vmem_limit_bytes=VMEM_LIMIT),
        name="rmsnorm",
    )(x, w.reshape(1, d))


def _mm_kernel(a_ref, b_ref, o_ref):
    o_ref[...] = jnp.dot(a_ref[...], b_ref[...], preferred_element_type=_F32).astype(o_ref.dtype)


def _mm_res_kernel(a_ref, b_ref, r_ref, o_ref):
    acc = jnp.dot(a_ref[...], b_ref[...], preferred_element_type=_F32)
    o_ref[...] = (acc + r_ref[...]).astype(o_ref.dtype)


def _matmul(a, b, *, tm, tn, out_dtype, res=None, name="matmul"):
    m, k = a.shape
    _, n = b.shape
    tm = min(tm, m)
    in_specs = [pl.BlockSpec((tm, k), lambda i, j: (i, 0)), pl.BlockSpec((k, tn), lambda i, j: (0, j))]
    args = [a, b]
    kern = _mm_kernel
    if res is not None:
        in_specs.append(pl.BlockSpec((tm, tn), lambda i, j: (i, j)))
        args.append(res)
        kern = _mm_res_kernel
    return pl.pallas_call(
        kern,
        grid=(m // tm, n // tn),
        in_specs=in_specs,
        out_specs=pl.BlockSpec((tm, tn), lambda i, j: (i, j)),
        out_shape=jax.ShapeDtypeStruct((m, n), out_dtype),
        compiler_params=pltpu.CompilerParams(dimension_semantics=("parallel", "arbitrary"),
                                             vmem_limit_bytes=VMEM_LIMIT),
        name=name,
    )(*args)


def _mm_wcast_kernel(a_ref, w_ref, o_ref, wbf_ref):
    @pl.when(pl.program_id(1) == 0)
    def _():
        wbf_ref[...] = w_ref[...].astype(_BF)

    o_ref[...] = jnp.dot(a_ref[...], wbf_ref[...], preferred_element_type=_F32).astype(o_ref.dtype)


def _mm_wcast_t_kernel(a_ref, wt_ref, o_ref, wbf_ref):
    @pl.when(pl.program_id(1) == 0)
    def _():
        wbf_ref[...] = wt_ref[...].astype(_BF)

    o_ref[...] = lax.dot_general(a_ref[...], wbf_ref[...], (((1,), (1,)), ((), ())),
                                 preferred_element_type=_F32).astype(o_ref.dtype)


def _matmul_wcast(a, w, layer, *, n_cols, tm, tn, out_dtype, name, transposed=False):
    m, k = a.shape
    tm = min(tm, m)
    if transposed:
        kern, w_spec, w_scr = (_mm_wcast_t_kernel, pl.BlockSpec((None, tn, k), lambda j, i: (layer, j, 0)),
                               pltpu.VMEM((tn, k), _BF))
    else:
        kern, w_spec, w_scr = (_mm_wcast_kernel, pl.BlockSpec((None, k, tn), lambda j, i: (layer, 0, j)),
                               pltpu.VMEM((k, tn), _BF))
    return pl.pallas_call(
        kern,
        grid=(n_cols // tn, m // tm),
        in_specs=[pl.BlockSpec((tm, k), lambda j, i: (i, 0)), w_spec],
        out_specs=pl.BlockSpec((tm, tn), lambda j, i: (i, j)),
        out_shape=jax.ShapeDtypeStruct((m, n_cols), out_dtype),
        scratch_shapes=[w_scr],
        compiler_params=pltpu.CompilerParams(dimension_semantics=("parallel", "arbitrary"),
                                             vmem_limit_bytes=VMEM_LIMIT),
        name=name,
    )(a, w)


def _hgrn_kernel(hq_ref, hf_ref, hi_ref, hg_ref, lb_ref, nw_ref, o_ref,
                 st_ref, q_s, k_s, b_s, *, heads):
    c = pl.program_id(1)

    @pl.when(c == 0)
    def _():
        st_ref[...] = jnp.zeros_like(st_ref)

    lb = lb_ref[...]
    f = lb + (1.0 - lb) * jax.nn.sigmoid(hf_ref[...])
    b_all = _cumsum_rows(jnp.log(f) * LOG2E, _tril_ones(CHUNK))
    q_all = _silu(hq_ref[...]) * (HEAD_DIM ** -0.5)
    k_all = 1.0 - f
    for h in range(heads):
        sl = slice(h * HEAD_DIM, (h + 1) * HEAD_DIM)
        q_s[h] = q_all[:, sl]
        k_s[h] = k_all[:, sl]
        b_s[h] = b_all[:, sl]

    nw = nw_ref[...]
    lane = lax.broadcasted_iota(jnp.int32, (SUB, CHUNK), 1)
    row = lax.broadcasted_iota(jnp.int32, (SUB, CHUNK), 0)
    r64 = lax.broadcasted_iota(jnp.int32, (CHUNK, CHUNK), 0)
    c64 = lax.broadcasted_iota(jnp.int32, (CHUNK, CHUNK), 1)
    _, _, level_masks = _doubling_masks(r64, c64)
    sub_log = SUB.bit_length() - 1
    level_masks = level_masks[sub_log - 1:]
    lower = r64 > c64

    v_bf, off, o_inter = [], [], []
    for h in range(heads):
        sl = slice(h * HEAD_DIM, (h + 1) * HEAD_DIM)
        q = q_s[h]
        k = k_s[h]
        b = b_s[h]
        v = hi_ref[:, sl].astype(_BF)
        st = st_ref[h]
        b_last = b_s[h, CHUNK - 1:CHUNK, :]
        acc = jnp.zeros((CHUNK, CHUNK), _F32)
        s = SUB
        for m in level_masks:
            ref = jnp.concatenate(
                [jnp.broadcast_to(b_s[h, g + s - 1:g + s, :], (2 * s, HEAD_DIM))
                 for g in range(0, CHUNK, 2 * s)], axis=0)
            ql = q * jnp.exp2(jnp.minimum(b - ref, 0.0))
            kl = k * jnp.exp2(jnp.minimum(ref - b, 0.0))
            acc = jnp.where(m & lower, _dot_nt(ql, kl), acc)
            s *= 2
        off.append(acc)
        o_inter.append(_dot_nt(q * jnp.exp2(b), st))
        kd = k * jnp.exp2(b_last - b)
        st_ref[h] = st * jnp.exp2(b_last) + _dot_tn(v, kd)
        v_bf.append(v)

    attn = []
    for h in range(heads):
        rows = []
        for blk in range(CHUNK // SUB):
            lo = blk * SUB
            qb = q_s[h, lo:lo + SUB, :]
            bb = b_s[h, lo:lo + SUB, :]
            acc = jnp.zeros((SUB, CHUNK), _F32)
            for s in range(SUB):
                e = jnp.exp2(jnp.minimum(bb - b_s[h, lo + s:lo + s + 1, :], 0.0))
                p = qb * e * k_s[h, lo + s:lo + s + 1, :]
                acc = jnp.where(lane == lo + s, jnp.sum(p, axis=-1, keepdims=True), acc)
            in_block = (lane >= lo) & (row + lo >= lane)
            rows.append(jnp.where(in_block, acc, off[h][lo:lo + SUB, :]))
        attn.append(jnp.concatenate(rows, axis=0).astype(_BF))

    o_intra = [_dot(attn[h], v_bf[h]) for h in range(heads)]
    for h in range(heads):
        sl = slice(h * HEAD_DIM, (h + 1) * HEAD_DIM)
        o = o_inter[h] + o_intra[h]
        ms = jnp.mean(o * o, axis=-1, keepdims=True)
        y = o * lax.rsqrt(ms + EPS) * nw * _silu(hg_ref[:, sl])
        o_ref[:, sl] = y.astype(o_ref.dtype)


def _hgrn(proj, lb, nw, *, batch, seq, col0, out_dtype=_BF):
    width = lb.shape[-1]
    heads = width // HEAD_DIM
    nc = seq // CHUNK
    cb = col0 // width

    def spec(off):
        return pl.BlockSpec((CHUNK, width), lambda bi, ci, off=off: (bi * nc + ci, cb + off))

    return pl.pallas_call(
        functools.partial(_hgrn_kernel, heads=heads),
        grid=(batch, nc),
        in_specs=[spec(0), spec(1), spec(2), spec(3),
                  pl.BlockSpec((1, width), lambda bi, ci: (0, 0)),
                  pl.BlockSpec((1, HEAD_DIM), lambda bi, ci: (0, 0))],
        out_specs=pl.BlockSpec((CHUNK, width), lambda bi, ci: (bi * nc + ci, 0)),
        out_shape=jax.ShapeDtypeStruct((batch * seq, width), out_dtype),
        scratch_shapes=[pltpu.VMEM((heads, HEAD_DIM, HEAD_DIM), _F32),
                        pltpu.VMEM((heads, CHUNK, HEAD_DIM), _F32),
                        pltpu.VMEM((heads, CHUNK, HEAD_DIM), _F32),
                        pltpu.VMEM((heads, CHUNK, HEAD_DIM), _F32)],
        compiler_params=pltpu.CompilerParams(dimension_semantics=("parallel", "arbitrary"),
                                             vmem_limit_bytes=VMEM_LIMIT),
        name="hgrn2",
    )(proj, proj, proj, proj, lb.reshape(1, width), nw.reshape(1, HEAD_DIM))


def _doubling_masks(r, c):
    eye = (r == c).astype(_F32)
    pair = (r >> 1) == (c >> 1)
    levels = []
    k = 1
    while (2 << k) <= CHUNK:
        levels.append(((r >> (k + 1)) == (c >> (k + 1))) & ((r >> k) != (c >> k)))
        k += 1
    return eye, pair, levels


def _unit_lower_inverses(mats, masks):
    eye, pair, levels = masks
    ts = [eye - jnp.where(pair, a, 0.0) for a in mats]
    for m in levels:
        xs = [_dot(jnp.where(m, a, 0.0), t) for a, t in zip(mats, ts)]
        ts = [t - _dot(t, x) for t, x in zip(ts, xs)]
    return ts


def _gdn_kernel(qkv_ref, z_ref, ba_ref, cw_ref, alog_ref, dtb_ref, nw_ref, o_ref,
                st_ref, stage_ref, *, k_heads, v_heads):
    c = pl.program_id(1)
    kw = k_heads * HEAD_DIM
    rep = v_heads // k_heads

    @pl.when(c == 0)
    def _():
        st_ref[...] = jnp.zeros_like(st_ref)
        stage_ref[0:SUBLANES, :] = jnp.zeros((SUBLANES, stage_ref.shape[1]), _F32)

    x = qkv_ref[...]
    stage_ref[SUBLANES:SUBLANES + CHUNK, :] = x
    cw = cw_ref[...]
    y = x * cw[GDN_CONV_WIDTH - 1:GDN_CONV_WIDTH, :]
    for j in range(GDN_CONV_WIDTH - 1):
        sh = GDN_CONV_WIDTH - 1 - j
        y = y + stage_ref[SUBLANES - sh:SUBLANES - sh + CHUNK, :] * cw[j:j + 1, :]
    stage_ref[0:SUBLANES, :] = x[CHUNK - SUBLANES:CHUNK, :]
    act = _silu(y)

    ba = ba_ref[...]
    beta_all = jax.nn.sigmoid(ba)
    xs = ba + dtb_ref[...]
    softplus = jnp.maximum(xs, 0.0) + jnp.log(1.0 + jnp.exp(-jnp.abs(xs)))
    g_all = -jnp.exp(alog_ref[...]) * softplus
    gc_all = _cumsum_rows(g_all, _tril_ones(CHUNK))
    beta_t = beta_all.T
    gc_t = gc_all.T

    r = lax.broadcasted_iota(jnp.int32, (CHUNK, CHUNK), 0)
    cc = lax.broadcasted_iota(jnp.int32, (CHUNK, CHUNK), 1)
    nw = nw_ref[...]
    masks = _doubling_masks(r, cc)

    hs = range(v_heads)
    qn, kn = [], []
    for kh in range(k_heads):
        qh = act[:, kh * HEAD_DIM:(kh + 1) * HEAD_DIM]
        kk_ = act[:, kw + kh * HEAD_DIM:kw + (kh + 1) * HEAD_DIM]
        qh = qh * lax.rsqrt(jnp.sum(qh * qh, axis=-1, keepdims=True) + EPS) * (HEAD_DIM ** -0.5)
        kk_ = kk_ * lax.rsqrt(jnp.sum(kk_ * kk_, axis=-1, keepdims=True) + EPS)
        qn.append(qh.astype(_BF))
        kn.append(kk_.astype(_BF))
    kkt = [_dot_nt(k_, k_) for k_ in kn]
    qkt = [_dot_nt(q_, k_) for q_, k_ in zip(qn, kn)]
    vh = [act[:, 2 * kw + h * HEAD_DIM:2 * kw + (h + 1) * HEAD_DIM].astype(_BF) for h in hs]

    bcol = [beta_all[:, h:h + 1] for h in hs]
    brow = [beta_t[h:h + 1, :] for h in hs]
    gcol = [gc_all[:, v_heads + h:v_heads + h + 1] for h in hs]
    grow = [gc_t[v_heads + h:v_heads + h + 1, :] for h in hs]
    glast = [g[CHUNK - 1:CHUNK, :] for g in gcol]
    decay = [jnp.where(r >= cc, jnp.exp(jnp.minimum(gcol[h] - grow[h], 0.0)), 0.0) for h in hs]
    amat = [jnp.where(r > cc, kkt[h // rep] * (bcol[h] * decay[h]), 0.0) for h in hs]
    tinv = _unit_lower_inverses(amat, masks)
    tb = [tinv[h] * brow[h] for h in hs]
    u = [_dot(tb[h], vh[h]) for h in hs]
    w = [_dot(tb[h] * jnp.exp(grow[h]), kn[h // rep]) for h in hs]
    qkl = [(qkt[h // rep] * decay[h]).astype(_BF) for h in hs]

    st = [st_ref[h] for h in hs]
    st_bf = [s_.astype(_BF) for s_ in st]
    ws = [_dot(w[h], st_bf[h]) for h in hs]
    qs = [_dot(qn[h // rep], st_bf[h]) for h in hs]
    v_new = [u[h] - ws[h] for h in hs]
    o_intra = [_dot(qkl[h], v_new[h]) for h in hs]
    kv = [_dot_tn(kn[h // rep], v_new[h] * jnp.exp(glast[h] - gcol[h])) for h in hs]
    for h in hs:
        st_ref[h] = st[h] * jnp.exp(glast[h]) + kv[h]
        o = jnp.exp(gcol[h]) * qs[h] + o_intra[h]
        ms = jnp.mean(o * o, axis=-1, keepdims=True)
        zh = z_ref[:, h * HEAD_DIM:(h + 1) * HEAD_DIM]
        yh = o * lax.rsqrt(ms + EPS) * nw * _silu(zh)
        o_ref[:, h * HEAD_DIM:(h + 1) * HEAD_DIM] = yh.astype(o_ref.dtype)


def _gdn(proj, ba, conv_w, alog_pad, dtb_pad, nw, *, batch, seq, qkv_col0, z_col0, k_heads, v_heads,
         out_dtype=_BF):
    qkv_w = (2 * k_heads + v_heads) * HEAD_DIM
    vw = v_heads * HEAD_DIM
    nc = seq // CHUNK
    qb = qkv_col0 // qkv_w
    zb = z_col0 // vw
    return pl.pallas_call(
        functools.partial(_gdn_kernel, k_heads=k_heads, v_heads=v_heads),
        grid=(batch, nc),
        in_specs=[pl.BlockSpec((CHUNK, qkv_w), lambda bi, ci: (bi * nc + ci, qb)),
                  pl.BlockSpec((CHUNK, vw), lambda bi, ci: (bi * nc + ci, zb)),
                  pl.BlockSpec((CHUNK, LANES), lambda bi, ci: (bi * nc + ci, 0)),
                  pl.BlockSpec((GDN_CONV_WIDTH, qkv_w), lambda bi, ci: (0, 0)),
                  pl.BlockSpec((1, LANES), lambda bi, ci: (0, 0)),
                  pl.BlockSpec((1, LANES), lambda bi, ci: (0, 0)),
                  pl.BlockSpec((1, HEAD_DIM), lambda bi, ci: (0, 0))],
        out_specs=pl.BlockSpec((CHUNK, vw), lambda bi, ci: (bi * nc + ci, 0)),
        out_shape=jax.ShapeDtypeStruct((batch * seq, vw), out_dtype),
        scratch_shapes=[pltpu.VMEM((v_heads, HEAD_DIM, HEAD_DIM), _F32),
                        pltpu.VMEM((SUBLANES + CHUNK, qkv_w), _F32)],
        compiler_params=pltpu.CompilerParams(dimension_semantics=("parallel", "arbitrary"),
                                             vmem_limit_bytes=VMEM_LIMIT),
        name="gated_delta",
    )(proj, proj, ba, conv_w, alog_pad, dtb_pad, nw.reshape(1, HEAD_DIM))


def _mm2_res_kernel(a1_ref, a2_ref, w_ref, r_ref, o_ref, wbf_ref):
    @pl.when(pl.program_id(1) == 0)
    def _():
        wbf_ref[...] = w_ref[...].astype(_BF)

    k1 = a1_ref.shape[1]
    acc = jnp.dot(a1_ref[...], wbf_ref[0:k1, :], preferred_element_type=_F32)
    acc = acc + jnp.dot(a2_ref[...], wbf_ref[k1:, :], preferred_element_type=_F32)
    o_ref[...] = (acc + r_ref[...]).astype(o_ref.dtype)


def _out_proj(a1, a2, w, layer, res, *, tm, tn):
    m, k1 = a1.shape
    _, k2 = a2.shape
    n = w.shape[2]
    tm = min(tm, m)
    return pl.pallas_call(
        _mm2_res_kernel,
        grid=(n // tn, m // tm),
        in_specs=[pl.BlockSpec((tm, k1), lambda j, i: (i, 0)),
                  pl.BlockSpec((tm, k2), lambda j, i: (i, 0)),
                  pl.BlockSpec((None, k1 + k2, tn), lambda j, i: (layer, 0, j)),
                  pl.BlockSpec((tm, tn), lambda j, i: (i, j))],
        out_specs=pl.BlockSpec((tm, tn), lambda j, i: (i, j)),
        out_shape=jax.ShapeDtypeStruct((m, n), _F32),
        scratch_shapes=[pltpu.VMEM((k1 + k2, tn), _BF)],
        compiler_params=pltpu.CompilerParams(dimension_semantics=("parallel", "arbitrary"),
                                             vmem_limit_bytes=VMEM_LIMIT),
        name="out_proj",
    )(a1, a2, w, res)


def _ffn_in_kernel(a_ref, wg_ref, wu_ref, cw_ref, cb_ref, o_ref, wg_bf, wu_bf, stage_ref, *, seq_tiles):
    i = pl.program_id(1)
    rows = a_ref.shape[0]

    @pl.when(i == 0)
    def _():
        wg_bf[...] = wg_ref[...].astype(_BF)
        wu_bf[...] = wu_ref[...].astype(_BF)

    @pl.when(i % seq_tiles == 0)
    def _():
        stage_ref[0:SUBLANES, :] = jnp.zeros((SUBLANES, stage_ref.shape[1]), _F32)

    a = a_ref[...]
    g = jnp.dot(a, wg_bf[...], preferred_element_type=_F32)
    u = jnp.dot(a, wu_bf[...], preferred_element_type=_F32)
    stage_ref[SUBLANES:SUBLANES + rows, :] = g
    cw = cw_ref[...]
    y = g * cw[FFN_CONV_WIDTH - 1:FFN_CONV_WIDTH, :] + cb_ref[...]
    for j in range(FFN_CONV_WIDTH - 1):
        sh = FFN_CONV_WIDTH - 1 - j
        y = y + stage_ref[SUBLANES - sh:SUBLANES - sh + rows, :] * cw[j:j + 1, :]
    stage_ref[0:SUBLANES, :] = g[rows - SUBLANES:rows, :]
    o_ref[...] = (_silu(y) * u).astype(o_ref.dtype)


def _ffn_in(a, w, layer, cw, cb, *, seq, tm, tn):
    m, k = a.shape
    d_ff = cb.shape[-1]
    tm = min(tm, seq)
    ncol = d_ff // tn
    return pl.pallas_call(
        functools.partial(_ffn_in_kernel, seq_tiles=seq // tm),
        grid=(ncol, m // tm),
        in_specs=[pl.BlockSpec((tm, k), lambda j, i: (i, 0)),
                  pl.BlockSpec((None, k, tn), lambda j, i: (layer, 0, j)),
                  pl.BlockSpec((None, k, tn), lambda j, i: (layer, 0, ncol + j)),
                  pl.BlockSpec((FFN_CONV_WIDTH, tn), lambda j, i: (0, j)),
                  pl.BlockSpec((1, tn), lambda j, i: (0, j))],
        out_specs=pl.BlockSpec((tm, tn), lambda j, i: (i, j)),
        out_shape=jax.ShapeDtypeStruct((m, d_ff), _BF),
        scratch_shapes=[pltpu.VMEM((k, tn), _BF), pltpu.VMEM((k, tn), _BF),
                        pltpu.VMEM((SUBLANES + tm, tn), _F32)],
        compiler_params=pltpu.CompilerParams(dimension_semantics=("parallel", "arbitrary"),
                                             vmem_limit_bytes=VMEM_LIMIT),
        name="ffn_in",
    )(a, w, w, cw, cb.reshape(1, d_ff))


def kernel(x, norm_mix_w, w_in, hgrn_lb_raw, hgrn_norm_w, gdn_conv_w, gdn_A_log, gdn_dt_bias, gdn_norm_w,
           w_out, norm_ffn_w, w_ffn_in, ffn_conv_w, ffn_conv_b, w_ffn_out, final_norm_w):
    batch, seq, d = x.shape
    depth = w_in.shape[0]
    v_heads = gdn_A_log.shape[1]
    gdn_w = v_heads * HEAD_DIM
    k_heads = (gdn_conv_w.shape[2] - gdn_w) // (2 * HEAD_DIM)
    qkv_w = gdn_conv_w.shape[2]
    hgrn_w = hgrn_lb_raw.shape[1]
    d_ff = ffn_conv_b.shape[1]
    main_w = 4 * hgrn_w + qkv_w + gdn_w
    t = batch * seq

    lb_all = jnp.cumsum(jax.nn.softmax(hgrn_lb_raw.astype(_F32), axis=0), axis=0)

    w_in_t = jnp.swapaxes(w_in, 1, 2)

    xt = x.reshape(t, d)
    for l in range(depth):
        w_gate = jnp.pad(w_in[l, :, main_w:], ((0, 0), (0, LANES - 2 * v_heads))).astype(_BF)
        pad_lo = jnp.zeros((v_heads,), _F32)
        pad_hi = jnp.zeros((LANES - 2 * v_heads,), _F32)
        alog_pad = jnp.concatenate([pad_lo, gdn_A_log[l].astype(_F32), pad_hi]).reshape(1, LANES)
        dtb_pad = jnp.concatenate([pad_lo, gdn_dt_bias[l].astype(_F32), pad_hi]).reshape(1, LANES)

        h = _rmsnorm(xt, norm_mix_w[l], _BF)
        proj = _matmul_wcast(h, w_in_t, l, n_cols=main_w, tm=512, tn=1024, out_dtype=_F32, name="in_proj",
                             transposed=True)
        ba = _matmul(h, w_gate, tm=1024, tn=LANES, out_dtype=_F32, name="in_proj_gates")

        o_h = _hgrn(proj, lb_all[l], hgrn_norm_w[l], batch=batch, seq=seq, col0=0)
        o_g = _gdn(proj, ba, gdn_conv_w[l], alog_pad, dtb_pad, gdn_norm_w[l], batch=batch, seq=seq,
                   qkv_col0=4 * hgrn_w, z_col0=4 * hgrn_w + qkv_w, k_heads=k_heads, v_heads=v_heads)
        xt = _out_proj(o_h, o_g, w_out, l, xt, tm=1024, tn=512)

        h2 = _rmsnorm(xt, norm_ffn_w[l], _BF)
        act = _ffn_in(h2, w_ffn_in, l, ffn_conv_w[l], ffn_conv_b[l], seq=seq, tm=1024, tn=256)
        xt = _matmul(act, w_ffn_out[l].astype(_BF), tm=512, tn=512, out_dtype=_F32, res=xt, name="ffn_out")

    out = _rmsnorm(xt, final_norm_w, x.dtype)
    return out.reshape(batch, seq, d)
```

```python
import functools

import jax
import jax.numpy as jnp
from jax import lax
from jax.experimental import pallas as pl
from jax.experimental.pallas import tpu as pltpu

HEAD_DIM = 128
CHUNK = 64
SUB = 8
LOG2E = 1.4426950408889634
GDN_STEP_CHUNKS = 2
GDN_CONV_WIDTH = 4
FFN_CONV_WIDTH = 3
EPS = 1e-6
LANES = 128
SUBLANES = 8
VMEM_LIMIT = 56 * 1024 * 1024

_BF = jnp.bfloat16
_F32 = jnp.float32


def _dot(a, b):
    return jnp.dot(a.astype(_BF), b.astype(_BF), preferred_element_type=_F32)


def _dot_nt(a, b):
    return lax.dot_general(a.astype(_BF), b.astype(_BF), (((1,), (1,)), ((), ())),
                           preferred_element_type=_F32)


def _dot_tn(a, b):
    return lax.dot_general(a.astype(_BF), b.astype(_BF), (((0,), (0,)), ((), ())),
                           preferred_element_type=_F32)


def _cumsum_rows(x, tri_bf):
    hi = x.astype(_BF)
    r1 = x - hi.astype(_F32)
    mid = r1.astype(_BF)
    lo = (r1 - mid.astype(_F32)).astype(_BF)
    acc = jnp.dot(tri_bf, hi, preferred_element_type=_F32)
    acc = acc + jnp.dot(tri_bf, mid, preferred_element_type=_F32)
    return acc + jnp.dot(tri_bf, lo, preferred_element_type=_F32)


def _tril_ones(n):
    r = lax.broadcasted_iota(jnp.int32, (n, n), 0)
    c = lax.broadcasted_iota(jnp.int32, (n, n), 1)
    return (r >= c).astype(_BF)


def _silu(x):
    return x * jax.nn.sigmoid(x)


def _rmsnorm_kernel(x_ref, w_ref, o_ref):
    x = x_ref[...]
    ms = jnp.mean(x * x, axis=-1, keepdims=True)
    o_ref[...] = (x * lax.rsqrt(ms + EPS) * w_ref[...]).astype(o_ref.dtype)


def _rmsnorm(x, w, out_dtype, tm=256):
    t, d = x.shape
    return pl.pallas_call(
        _rmsnorm_kernel,
        grid=(t // tm,),
        in_specs=[pl.BlockSpec((tm, d), lambda i: (i, 0)), pl.BlockSpec((1, d), lambda i: (0, 0))],
        out_specs=pl.BlockSpec((tm, d), lambda i: (i, 0)),
        out_shape=jax.ShapeDtypeStruct((t, d), out_dtype),
        compiler_params=pltpu.CompilerParams(dimension_semantics=("parallel",),
                                             vmem_limit_bytes=VMEM_LIMIT),
        name="rmsnorm",
    )(x, w.reshape(1, d))


def _mm_kernel(a_ref, b_ref, o_ref):
    o_ref[...] = jnp.dot(a_ref[...], b_ref[...], preferred_element_type=_F32).astype(o_ref.dtype)


def _mm_res_kernel(a_ref, b_ref, r_ref, o_ref):
    acc = jnp.dot(a_ref[...], b_ref[...], preferred_element_type=_F32)
    o_ref[...] = (acc + r_ref[...]).astype(o_ref.dtype)


def _matmul(a, b, *, tm, tn, out_dtype, res=None, name="matmul"):
    m, k = a.shape
    _, n = b.shape
    tm = min(tm, m)
    in_specs = [pl.BlockSpec((tm, k), lambda i, j: (i, 0)), pl.BlockSpec((k, tn), lambda i, j: (0, j))]
    args = [a, b]
    kern = _mm_kernel
    if res is not None:
        in_specs.append(pl.BlockSpec((tm, tn), lambda i, j: (i, j)))
        args.append(res)
        kern = _mm_res_kernel
    return pl.pallas_call(
        kern,
        grid=(m // tm, n // tn),
        in_specs=in_specs,
        out_specs=pl.BlockSpec((tm, tn), lambda i, j: (i, j)),
        out_shape=jax.ShapeDtypeStruct((m, n), out_dtype),
        compiler_params=pltpu.CompilerParams(dimension_semantics=("parallel", "arbitrary"),
                                             vmem_limit_bytes=VMEM_LIMIT),
        name=name,
    )(*args)


def _mm_wcast_kernel(a_ref, w_ref, o_ref, wbf_ref):
    @pl.when(pl.program_id(1) == 0)
    def _():
        wbf_ref[...] = w_ref[...].astype(_BF)

    o_ref[...] = jnp.dot(a_ref[...], wbf_ref[...], preferred_element_type=_F32).astype(o_ref.dtype)


def _mm_wcast_t_kernel(a_ref, wt_ref, o_ref, wbf_ref):
    @pl.when(pl.program_id(1) == 0)
    def _():
        wbf_ref[...] = wt_ref[...].astype(_BF)

    o_ref[...] = lax.dot_general(a_ref[...], wbf_ref[...], (((1,), (1,)), ((), ())),
                                 preferred_element_type=_F32).astype(o_ref.dtype)


def _matmul_wcast(a, w, layer, *, n_cols, tm, tn, out_dtype, name, transposed=False):
    m, k = a.shape
    tm = min(tm, m)
    if transposed:
        kern, w_spec, w_scr = (_mm_wcast_t_kernel, pl.BlockSpec((None, tn, k), lambda j, i: (layer, j, 0)),
                               pltpu.VMEM((tn, k), _BF))
    else:
        kern, w_spec, w_scr = (_mm_wcast_kernel, pl.BlockSpec((None, k, tn), lambda j, i: (layer, 0, j)),
                               pltpu.VMEM((k, tn), _BF))
    return pl.pallas_call(
        kern,
        grid=(n_cols // tn, m // tm),
        in_specs=[pl.BlockSpec((tm, k), lambda j, i: (i, 0)), w_spec],
        out_specs=pl.BlockSpec((tm, tn), lambda j, i: (i, j)),
        out_shape=jax.ShapeDtypeStruct((m, n_cols), out_dtype),
        scratch_shapes=[w_scr],
        compiler_params=pltpu.CompilerParams(dimension_semantics=("parallel", "arbitrary"),
                                             vmem_limit_bytes=VMEM_LIMIT),
        name=name,
    )(a, w)


def _hgrn_kernel(hq_ref, hf_ref, hi_ref, hg_ref, lb_ref, nw_ref, o_ref,
                 st_ref, q_s, k_s, b_s, *, heads):
    c = pl.program_id(1)

    @pl.when(c == 0)
    def _():
        st_ref[...] = jnp.zeros_like(st_ref)

    lb = lb_ref[...]
    f = lb + (1.0 - lb) * jax.nn.sigmoid(hf_ref[...])
    b_all = _cumsum_rows(jnp.log(f) * LOG2E, _tril_ones(CHUNK))
    q_all = _silu(hq_ref[...]) * (HEAD_DIM ** -0.5)
    k_all = 1.0 - f
    for h in range(heads):
        sl = slice(h * HEAD_DIM, (h + 1) * HEAD_DIM)
        q_s[h] = q_all[:, sl]
        k_s[h] = k_all[:, sl]
        b_s[h] = b_all[:, sl]

    nw = nw_ref[...]
    lane = lax.broadcasted_iota(jnp.int32, (SUB, CHUNK), 1)
    row = lax.broadcasted_iota(jnp.int32, (SUB, CHUNK), 0)
    r64 = lax.broadcasted_iota(jnp.int32, (CHUNK, CHUNK), 0)
    c64 = lax.broadcasted_iota(jnp.int32, (CHUNK, CHUNK), 1)
    _, _, level_masks = _doubling_masks(r64, c64)
    sub_log = SUB.bit_length() - 1
    level_masks = level_masks[sub_log - 1:]
    lower = r64 > c64

    v_bf, off, o_inter = [], [], []
    for h in range(heads):
        sl = slice(h * HEAD_DIM, (h + 1) * HEAD_DIM)
        q = q_s[h]
        k = k_s[h]
        b = b_s[h]
        v = hi_ref[:, sl].astype(_BF)
        st = st_ref[h]
        b_last = b_s[h, CHUNK - 1:CHUNK, :]
        acc = jnp.zeros((CHUNK, CHUNK), _F32)
        s = SUB
        for m in level_masks:
            ref = jnp.concatenate(
                [jnp.broadcast_to(b_s[h, g + s - 1:g + s, :], (2 * s, HEAD_DIM))
                 for g in range(0, CHUNK, 2 * s)], axis=0)
            ql = q * jnp.exp2(jnp.minimum(b - ref, 0.0))
            kl = k * jnp.exp2(jnp.minimum(ref - b, 0.0))
            acc = jnp.where(m & lower, _dot_nt(ql, kl), acc)
            s *= 2
        off.append(acc)
        o_inter.append(_dot_nt(q * jnp.exp2(b), st))
        kd = k * jnp.exp2(b_last - b)
        st_ref[h] = st * jnp.exp2(b_last) + _dot_tn(v, kd)
        v_bf.append(v)

    attn = []
    for h in range(heads):
        rows = []
        for blk in range(CHUNK // SUB):
            lo = blk * SUB
            qb = q_s[h, lo:lo + SUB, :]
            bb = b_s[h, lo:lo + SUB, :]
            acc = jnp.zeros((SUB, CHUNK), _F32)
            for s in range(SUB):
                e = jnp.exp2(jnp.minimum(bb - b_s[h, lo + s:lo + s + 1, :], 0.0))
                p = qb * e * k_s[h, lo + s:lo + s + 1, :]
                acc = jnp.where(lane == lo + s, jnp.sum(p, axis=-1, keepdims=True), acc)
            in_block = (lane >= lo) & (row + lo >= lane)
            rows.append(jnp.where(in_block, acc, off[h][lo:lo + SUB, :]))
        attn.append(jnp.concatenate(rows, axis=0).astype(_BF))

    o_intra = [_dot(attn[h], v_bf[h]) for h in range(heads)]
    for h in range(heads):
        sl = slice(h * HEAD_DIM, (h + 1) * HEAD_DIM)
        o = o_inter[h] + o_intra[h]
        ms = jnp.mean(o * o, axis=-1, keepdims=True)
        y = o * lax.rsqrt(ms + EPS) * nw * _silu(hg_ref[:, sl])
        o_ref[:, sl] = y.astype(o_ref.dtype)


def _hgrn(proj, lb, nw, *, batch, seq, col0, out_dtype=_BF):
    width = lb.shape[-1]
    heads = width // HEAD_DIM
    nc = seq // CHUNK
    cb = col0 // width

    def spec(off):
        return pl.BlockSpec((CHUNK, width), lambda bi, ci, off=off: (bi * nc + ci, cb + off))

    return pl.pallas_call(
        functools.partial(_hgrn_kernel, heads=heads),
        grid=(batch, nc),
        in_specs=[spec(0), spec(1), spec(2), spec(3),
                  pl.BlockSpec((1, width), lambda bi, ci: (0, 0)),
                  pl.BlockSpec((1, HEAD_DIM), lambda bi, ci: (0, 0))],
        out_specs=pl.BlockSpec((CHUNK, width), lambda bi, ci: (bi * nc + ci, 0)),
        out_shape=jax.ShapeDtypeStruct((batch * seq, width), out_dtype),
        scratch_shapes=[pltpu.VMEM((heads, HEAD_DIM, HEAD_DIM), _F32),
                        pltpu.VMEM((heads, CHUNK, HEAD_DIM), _F32),
                        pltpu.VMEM((heads, CHUNK, HEAD_DIM), _F32),
                        pltpu.VMEM((heads, CHUNK, HEAD_DIM), _F32)],
        compiler_params=pltpu.CompilerParams(dimension_semantics=("parallel", "arbitrary"),
                                             vmem_limit_bytes=VMEM_LIMIT),
        name="hgrn2",
    )(proj, proj, proj, proj, lb.reshape(1, width), nw.reshape(1, HEAD_DIM))


def _doubling_masks(r, c):
    eye = (r == c).astype(_F32)
    pair = (r >> 1) == (c >> 1)
    levels = []
    k = 1
    while (2 << k) <= CHUNK:
        levels.append(((r >> (k + 1)) == (c >> (k + 1))) & ((r >> k) != (c >> k)))
        k += 1
    return eye, pair, levels


def _unit_lower_inverses(mats, masks):
    eye, pair, levels = masks
    ts = [eye - jnp.where(pair, a, 0.0) for a in mats]
    for m in levels:
        xs = [_dot(jnp.where(m, a, 0.0), t) for a, t in zip(mats, ts)]
        ts = [t - _dot(t, x) for t, x in zip(ts, xs)]
    return ts


def _gdn_kernel(qkv_ref, z_ref, ba_ref, cw_ref, alog_ref, dtb_ref, nw_ref, o_ref,
                st_ref, stage_ref, *, k_heads, v_heads):
    c = pl.program_id(1)
    kw = k_heads * HEAD_DIM
    rep = v_heads // k_heads

    @pl.when(c == 0)
    def _():
        st_ref[...] = jnp.zeros_like(st_ref)
        stage_ref[0:SUBLANES, :] = jnp.zeros((SUBLANES, stage_ref.shape[1]), _F32)

    rows = qkv_ref.shape[0]
    n_chunks = rows // CHUNK
    x = qkv_ref[...]
    stage_ref[SUBLANES:SUBLANES + rows, :] = x
    cw = cw_ref[...]
    y = x * cw[GDN_CONV_WIDTH - 1:GDN_CONV_WIDTH, :]
    for j in range(GDN_CONV_WIDTH - 1):
        sh = GDN_CONV_WIDTH - 1 - j
        y = y + stage_ref[SUBLANES - sh:SUBLANES - sh + rows, :] * cw[j:j + 1, :]
    stage_ref[0:SUBLANES, :] = x[rows - SUBLANES:rows, :]
    act = _silu(y)

    r = lax.broadcasted_iota(jnp.int32, (CHUNK, CHUNK), 0)
    cc = lax.broadcasted_iota(jnp.int32, (CHUNK, CHUNK), 1)
    nw = nw_ref[...]
    masks = _doubling_masks(r, cc)
    tri = _tril_ones(CHUNK)
    hs = range(v_heads)

    qn, kn, vh, gcol, glast, qkl, amat, brow, grow = [], [], [], [], [], [], [], [], []
    for ci in range(n_chunks):
        rs = slice(ci * CHUNK, (ci + 1) * CHUNK)
        ba = ba_ref[rs, :]
        beta_all = jax.nn.sigmoid(ba)
        xs = ba + dtb_ref[...]
        softplus = jnp.maximum(xs, 0.0) + jnp.log(1.0 + jnp.exp(-jnp.abs(xs)))
        gc_all = _cumsum_rows(-jnp.exp(alog_ref[...]) * softplus, tri)
        beta_t = beta_all.T
        gc_t = gc_all.T
        q_c, k_c = [], []
        for kh in range(k_heads):
            qh = act[rs, kh * HEAD_DIM:(kh + 1) * HEAD_DIM]
            kk_ = act[rs, kw + kh * HEAD_DIM:kw + (kh + 1) * HEAD_DIM]
            qh = qh * lax.rsqrt(jnp.sum(qh * qh, axis=-1, keepdims=True) + EPS) * (HEAD_DIM ** -0.5)
            kk_ = kk_ * lax.rsqrt(jnp.sum(kk_ * kk_, axis=-1, keepdims=True) + EPS)
            q_c.append(qh.astype(_BF))
            k_c.append(kk_.astype(_BF))
        kkt = [_dot_nt(k_, k_) for k_ in k_c]
        qkt = [_dot_nt(q_, k_) for q_, k_ in zip(q_c, k_c)]
        for h in hs:
            bcol = beta_all[:, h:h + 1]
            gc = gc_all[:, v_heads + h:v_heads + h + 1]
            gr = gc_t[v_heads + h:v_heads + h + 1, :]
            dec = jnp.where(r >= cc, jnp.exp(jnp.minimum(gc - gr, 0.0)), 0.0)
            qn.append(q_c[h // rep])
            kn.append(k_c[h // rep])
            vh.append(act[rs, 2 * kw + h * HEAD_DIM:2 * kw + (h + 1) * HEAD_DIM].astype(_BF))
            gcol.append(gc)
            glast.append(gc[CHUNK - 1:CHUNK, :])
            brow.append(beta_t[h:h + 1, :])
            grow.append(gr)
            qkl.append((qkt[h // rep] * dec).astype(_BF))
            amat.append(jnp.where(r > cc, kkt[h // rep] * (bcol * dec), 0.0))
    n = len(amat)
    tinv = _unit_lower_inverses(amat, masks)
    tb = [tinv[i] * brow[i] for i in range(n)]
    u = [_dot(tb[i], vh[i]) for i in range(n)]
    w = [_dot(tb[i] * jnp.exp(grow[i]), kn[i]) for i in range(n)]

    st = [st_ref[h] for h in hs]
    for ci in range(n_chunks):
        rs = slice(ci * CHUNK, (ci + 1) * CHUNK)
        ix = [ci * v_heads + h for h in hs]
        st_bf = [s_.astype(_BF) for s_ in st]
        ws = [_dot(w[ix[h]], st_bf[h]) for h in hs]
        qs = [_dot(qn[ix[h]], st_bf[h]) for h in hs]
        v_new = [u[ix[h]] - ws[h] for h in hs]
        o_intra = [_dot(qkl[ix[h]], v_new[h]) for h in hs]
        kv = [_dot_tn(kn[ix[h]], v_new[h] * jnp.exp(glast[ix[h]] - gcol[ix[h]])) for h in hs]
        st = [st[h] * jnp.exp(glast[ix[h]]) + kv[h] for h in hs]
        for h in hs:
            o = jnp.exp(gcol[ix[h]]) * qs[h] + o_intra[h]
            ms = jnp.mean(o * o, axis=-1, keepdims=True)
            zh = z_ref[rs, h * HEAD_DIM:(h + 1) * HEAD_DIM]
            yh = o * lax.rsqrt(ms + EPS) * nw * _silu(zh)
            o_ref[rs, h * HEAD_DIM:(h + 1) * HEAD_DIM] = yh.astype(o_ref.dtype)
    for h in hs:
        st_ref[h] = st[h]


def _gdn(proj, ba, conv_w, alog_pad, dtb_pad, nw, *, batch, seq, qkv_col0, z_col0, k_heads, v_heads,
         out_dtype=_BF):
    qkv_w = (2 * k_heads + v_heads) * HEAD_DIM
    vw = v_heads * HEAD_DIM
    rows = GDN_STEP_CHUNKS * CHUNK
    nc = seq // rows
    qb = qkv_col0 // qkv_w
    zb = z_col0 // vw
    return pl.pallas_call(
        functools.partial(_gdn_kernel, k_heads=k_heads, v_heads=v_heads),
        grid=(batch, nc),
        in_specs=[pl.BlockSpec((rows, qkv_w), lambda bi, ci: (bi * nc + ci, qb)),
                  pl.BlockSpec((rows, vw), lambda bi, ci: (bi * nc + ci, zb)),
                  pl.BlockSpec((rows, LANES), lambda bi, ci: (bi * nc + ci, 0)),
                  pl.BlockSpec((GDN_CONV_WIDTH, qkv_w), lambda bi, ci: (0, 0)),
                  pl.BlockSpec((1, LANES), lambda bi, ci: (0, 0)),
                  pl.BlockSpec((1, LANES), lambda bi, ci: (0, 0)),
                  pl.BlockSpec((1, HEAD_DIM), lambda bi, ci: (0, 0))],
        out_specs=pl.BlockSpec((rows, vw), lambda bi, ci: (bi * nc + ci, 0)),
        out_shape=jax.ShapeDtypeStruct((batch * seq, vw), out_dtype),
        scratch_shapes=[pltpu.VMEM((v_heads, HEAD_DIM, HEAD_DIM), _F32),
                        pltpu.VMEM((SUBLANES + rows, qkv_w), _F32)],
        compiler_params=pltpu.CompilerParams(dimension_semantics=("parallel", "arbitrary"),
                                             vmem_limit_bytes=VMEM_LIMIT),
        name="gated_delta",
    )(proj, proj, ba, conv_w, alog_pad, dtb_pad, nw.reshape(1, HEAD_DIM))


def _mm2_res_kernel(a1_ref, a2_ref, w_ref, r_ref, o_ref, wbf_ref):
    @pl.when(pl.program_id(1) == 0)
    def _():
        wbf_ref[...] = w_ref[...].astype(_BF)

    k1 = a1_ref.shape[1]
    acc = jnp.dot(a1_ref[...], wbf_ref[0:k1, :], preferred_element_type=_F32)
    acc = acc + jnp.dot(a2_ref[...], wbf_ref[k1:, :], preferred_element_type=_F32)
    o_ref[...] = (acc + r_ref[...]).astype(o_ref.dtype)


def _out_proj(a1, a2, w, layer, res, *, tm, tn):
    m, k1 = a1.shape
    _, k2 = a2.shape
    n = w.shape[2]
    tm = min(tm, m)
    return pl.pallas_call(
        _mm2_res_kernel,
        grid=(n // tn, m // tm),
        in_specs=[pl.BlockSpec((tm, k1), lambda j, i: (i, 0)),
                  pl.BlockSpec((tm, k2), lambda j, i: (i, 0)),
                  pl.BlockSpec((None, k1 + k2, tn), lambda j, i: (layer, 0, j)),
                  pl.BlockSpec((tm, tn), lambda j, i: (i, j))],
        out_specs=pl.BlockSpec((tm, tn), lambda j, i: (i, j)),
        out_shape=jax.ShapeDtypeStruct((m, n), _F32),
        scratch_shapes=[pltpu.VMEM((k1 + k2, tn), _BF)],
        compiler_params=pltpu.CompilerParams(dimension_semantics=("parallel", "arbitrary"),
                                             vmem_limit_bytes=VMEM_LIMIT),
        name="out_proj",
    )(a1, a2, w, res)


def _ffn_in_kernel(a_ref, wg_ref, wu_ref, cw_ref, cb_ref, o_ref, wg_bf, wu_bf, stage_ref, *, seq_tiles):
    i = pl.program_id(1)
    rows = a_ref.shape[0]

    @pl.when(i == 0)
    def _():
        wg_bf[...] = wg_ref[...].astype(_BF)
        wu_bf[...] = wu_ref[...].astype(_BF)

    @pl.when(i % seq_tiles == 0)
    def _():
        stage_ref[0:SUBLANES, :] = jnp.zeros((SUBLANES, stage_ref.shape[1]), _F32)

    a = a_ref[...]
    g = jnp.dot(a, wg_bf[...], preferred_element_type=_F32)
    u = jnp.dot(a, wu_bf[...], preferred_element_type=_F32)
    stage_ref[SUBLANES:SUBLANES + rows, :] = g
    cw = cw_ref[...]
    y = g * cw[FFN_CONV_WIDTH - 1:FFN_CONV_WIDTH, :] + cb_ref[...]
    for j in range(FFN_CONV_WIDTH - 1):
        sh = FFN_CONV_WIDTH - 1 - j
        y = y + stage_ref[SUBLANES - sh:SUBLANES - sh + rows, :] * cw[j:j + 1, :]
    stage_ref[0:SUBLANES, :] = g[rows - SUBLANES:rows, :]
    o_ref[...] = (_silu(y) * u).astype(o_ref.dtype)


def _ffn_in(a, w, layer, cw, cb, *, seq, tm, tn):
    m, k = a.shape
    d_ff = cb.shape[-1]
    tm = min(tm, seq)
    ncol = d_ff // tn
    return pl.pallas_call(
        functools.partial(_ffn_in_kernel, seq_tiles=seq // tm),
        grid=(ncol, m // tm),
        in_specs=[pl.BlockSpec((tm, k), lambda j, i: (i, 0)),
                  pl.BlockSpec((None, k, tn), lambda j, i: (layer, 0, j)),
                  pl.BlockSpec((None, k, tn), lambda j, i: (layer, 0, ncol + j)),
                  pl.BlockSpec((FFN_CONV_WIDTH, tn), lambda j, i: (0, j)),
                  pl.BlockSpec((1, tn), lambda j, i: (0, j))],
        out_specs=pl.BlockSpec((tm, tn), lambda j, i: (i, j)),
        out_shape=jax.ShapeDtypeStruct((m, d_ff), _BF),
        scratch_shapes=[pltpu.VMEM((k, tn), _BF), pltpu.VMEM((k, tn), _BF),
                        pltpu.VMEM((SUBLANES + tm, tn), _F32)],
        compiler_params=pltpu.CompilerParams(dimension_semantics=("parallel", "arbitrary"),
                                             vmem_limit_bytes=VMEM_LIMIT),
        name="ffn_in",
    )(a, w, w, cw, cb.reshape(1, d_ff))


def kernel(x, norm_mix_w, w_in, hgrn_lb_raw, hgrn_norm_w, gdn_conv_w, gdn_A_log, gdn_dt_bias, gdn_norm_w,
           w_out, norm_ffn_w, w_ffn_in, ffn_conv_w, ffn_conv_b, w_ffn_out, final_norm_w):
    batch, seq, d = x.shape
    depth = w_in.shape[0]
    v_heads = gdn_A_log.shape[1]
    gdn_w = v_heads * HEAD_DIM
    k_heads = (gdn_conv_w.shape[2] - gdn_w) // (2 * HEAD_DIM)
    qkv_w = gdn_conv_w.shape[2]
    hgrn_w = hgrn_lb_raw.shape[1]
    d_ff = ffn_conv_b.shape[1]
    main_w = 4 * hgrn_w + qkv_w + gdn_w
    t = batch * seq

    lb_all = jnp.cumsum(jax.nn.softmax(hgrn_lb_raw.astype(_F32), axis=0), axis=0)

    w_in_t = jnp.swapaxes(w_in, 1, 2)

    xt = x.reshape(t, d)
    for l in range(depth):
        w_gate = jnp.pad(w_in[l, :, main_w:], ((0, 0), (0, LANES - 2 * v_heads))).astype(_BF)
        pad_lo = jnp.zeros((v_heads,), _F32)
        pad_hi = jnp.zeros((LANES - 2 * v_heads,), _F32)
        alog_pad = jnp.concatenate([pad_lo, gdn_A_log[l].astype(_F32), pad_hi]).reshape(1, LANES)
        dtb_pad = jnp.concatenate([pad_lo, gdn_dt_bias[l].astype(_F32), pad_hi]).reshape(1, LANES)

        h = _rmsnorm(xt, norm_mix_w[l], _BF)
        proj = _matmul_wcast(h, w_in_t, l, n_cols=main_w, tm=512, tn=1024, out_dtype=_F32, name="in_proj",
                             transposed=True)
        ba = _matmul(h, w_gate, tm=1024, tn=LANES, out_dtype=_F32, name="in_proj_gates")

        o_h = _hgrn(proj, lb_all[l], hgrn_norm_w[l], batch=batch, seq=seq, col0=0)
        o_g = _gdn(proj, ba, gdn_conv_w[l], alog_pad, dtb_pad, gdn_norm_w[l], batch=batch, seq=seq,
                   qkv_col0=4 * hgrn_w, z_col0=4 * hgrn_w + qkv_w, k_heads=k_heads, v_heads=v_heads)
        xt = _out_proj(o_h, o_g, w_out, l, xt, tm=1024, tn=512)

        h2 = _rmsnorm(xt, norm_ffn_w[l], _BF)
        act = _ffn_in(h2, w_ffn_in, l, ffn_conv_w[l], ffn_conv_b[l], seq=seq, tm=1024, tn=256)
        xt = _matmul(act, w_ffn_out[l].astype(_BF), tm=512, tn=512, out_dtype=_F32, res=xt, name="ffn_out")

    out = _rmsnorm(xt, final_norm_w, x.dtype)
    return out.reshape(batch, seq, d)
```

```python
import functools

import jax
import jax.numpy as jnp
from jax import lax
from jax.experimental import pallas as pl
from jax.experimental.pallas import tpu as pltpu

HEAD_DIM = 128
CHUNK = 64
SUB = 8
LOG2E = 1.4426950408889634
GDN_STEP_CHUNKS = 2
GDN_CONV_WIDTH = 4
FFN_CONV_WIDTH = 3
EPS = 1e-6
LANES = 128
SUBLANES = 8
VMEM_LIMIT = 56 * 1024 * 1024

TILES = {
    "in_proj": (512, 1024),
    "in_proj_gates": (1024, LANES),
    "out_proj": (1024, 512),
    "ffn_in": (1024, 256),
    "ffn_out": (512, 512),
}

_BF = jnp.bfloat16
_F32 = jnp.float32


def _dot(a, b):
    return jnp.dot(a.astype(_BF), b.astype(_BF), preferred_element_type=_F32)


def _dot_nt(a, b):
    return lax.dot_general(a.astype(_BF), b.astype(_BF), (((1,), (1,)), ((), ())),
                           preferred_element_type=_F32)


def _dot_tn(a, b):
    return lax.dot_general(a.astype(_BF), b.astype(_BF), (((0,), (0,)), ((), ())),
                           preferred_element_type=_F32)


def _cumsum_rows(x, tri_bf):
    hi = x.astype(_BF)
    r1 = x - hi.astype(_F32)
    mid = r1.astype(_BF)
    lo = (r1 - mid.astype(_F32)).astype(_BF)
    acc = jnp.dot(tri_bf, hi, preferred_element_type=_F32)
    acc = acc + jnp.dot(tri_bf, mid, preferred_element_type=_F32)
    return acc + jnp.dot(tri_bf, lo, preferred_element_type=_F32)


def _tril_ones(n):
    r = lax.broadcasted_iota(jnp.int32, (n, n), 0)
    c = lax.broadcasted_iota(jnp.int32, (n, n), 1)
    return (r >= c).astype(_BF)


def _silu(x):
    return x * jax.nn.sigmoid(x)


def _rmsnorm_kernel(x_ref, w_ref, o_ref):
    x = x_ref[...]
    ms = jnp.mean(x * x, axis=-1, keepdims=True)
    o_ref[...] = (x * lax.rsqrt(ms + EPS) * w_ref[...]).astype(o_ref.dtype)


def _rmsnorm(x, w, out_dtype, tm=512):
    t, d = x.shape
    return pl.pallas_call(
        _rmsnorm_kernel,
        grid=(t // tm,),
        in_specs=[pl.BlockSpec((tm, d), lambda i: (i, 0)), pl.BlockSpec((1, d), lambda i: (0, 0))],
        out_specs=pl.BlockSpec((tm, d), lambda i: (i, 0)),
        out_shape=jax.ShapeDtypeStruct((t, d), out_dtype),
        compiler_params=pltpu.CompilerParams(dimension_semantics=("parallel",),
                                             vmem_limit_bytes=VMEM_LIMIT),
        name="rmsnorm",
    )(x, w.reshape(1, d))


def _mm_kernel(a_ref, b_ref, o_ref):
    o_ref[...] = jnp.dot(a_ref[...], b_ref[...], preferred_element_type=_F32).astype(o_ref.dtype)


def _mm_res_kernel(a_ref, b_ref, r_ref, o_ref):
    acc = jnp.dot(a_ref[...], b_ref[...], preferred_element_type=_F32)
    o_ref[...] = (acc + r_ref[...]).astype(o_ref.dtype)


def _matmul(a, b, *, tm, tn, out_dtype, res=None, name="matmul"):
    m, k = a.shape
    _, n = b.shape
    tm = min(tm, m)
    in_specs = [pl.BlockSpec((tm, k), lambda i, j: (i, 0)), pl.BlockSpec((k, tn), lambda i, j: (0, j))]
    args = [a, b]
    kern = _mm_kernel
    if res is not None:
        in_specs.append(pl.BlockSpec((tm, tn), lambda i, j: (i, j)))
        args.append(res)
        kern = _mm_res_kernel
    return pl.pallas_call(
        kern,
        grid=(m // tm, n // tn),
        in_specs=in_specs,
        out_specs=pl.BlockSpec((tm, tn), lambda i, j: (i, j)),
        out_shape=jax.ShapeDtypeStruct((m, n), out_dtype),
        compiler_params=pltpu.CompilerParams(dimension_semantics=("parallel", "arbitrary"),
                                             vmem_limit_bytes=VMEM_LIMIT),
        name=name,
    )(*args)


def _mm_wcast_kernel(a_ref, w_ref, o_ref, wbf_ref):
    @pl.when(pl.program_id(1) == 0)
    def _():
        wbf_ref[...] = w_ref[...].astype(_BF)

    o_ref[...] = jnp.dot(a_ref[...], wbf_ref[...], preferred_element_type=_F32).astype(o_ref.dtype)


def _mm_wcast_t_kernel(a_ref, wt_ref, o_ref, wbf_ref):
    @pl.when(pl.program_id(1) == 0)
    def _():
        wbf_ref[...] = wt_ref[...].astype(_BF)

    o_ref[...] = lax.dot_general(a_ref[...], wbf_ref[...], (((1,), (1,)), ((), ())),
                                 preferred_element_type=_F32).astype(o_ref.dtype)


def _matmul_wcast(a, w, layer, *, n_cols, tm, tn, out_dtype, name, transposed=False):
    m, k = a.shape
    tm = min(tm, m)
    if transposed:
        kern, w_spec, w_scr = (_mm_wcast_t_kernel, pl.BlockSpec((None, tn, k), lambda j, i: (layer, j, 0)),
                               pltpu.VMEM((tn, k), _BF))
    else:
        kern, w_spec, w_scr = (_mm_wcast_kernel, pl.BlockSpec((None, k, tn), lambda j, i: (layer, 0, j)),
                               pltpu.VMEM((k, tn), _BF))
    return pl.pallas_call(
        kern,
        grid=(n_cols // tn, m // tm),
        in_specs=[pl.BlockSpec((tm, k), lambda j, i: (i, 0)), w_spec],
        out_specs=pl.BlockSpec((tm, tn), lambda j, i: (i, j)),
        out_shape=jax.ShapeDtypeStruct((m, n_cols), out_dtype),
        scratch_shapes=[w_scr],
        compiler_params=pltpu.CompilerParams(dimension_semantics=("parallel", "arbitrary"),
                                             vmem_limit_bytes=VMEM_LIMIT),
        name=name,
    )(a, w)


def _hgrn_kernel(hq_ref, hf_ref, hi_ref, hg_ref, lb_ref, nw_ref, o_ref,
                 st_ref, q_s, k_s, b_s, *, heads):
    c = pl.program_id(1)

    @pl.when(c == 0)
    def _():
        st_ref[...] = jnp.zeros_like(st_ref)

    lb = lb_ref[...]
    f = lb + (1.0 - lb) * jax.nn.sigmoid(hf_ref[...])
    b_all = _cumsum_rows(jnp.log(f) * LOG2E, _tril_ones(CHUNK))
    q_all = _silu(hq_ref[...]) * (HEAD_DIM ** -0.5)
    k_all = 1.0 - f
    for h in range(heads):
        sl = slice(h * HEAD_DIM, (h + 1) * HEAD_DIM)
        q_s[h] = q_all[:, sl]
        k_s[h] = k_all[:, sl]
        b_s[h] = b_all[:, sl]

    nw = nw_ref[...]
    lane = lax.broadcasted_iota(jnp.int32, (SUB, CHUNK), 1)
    row = lax.broadcasted_iota(jnp.int32, (SUB, CHUNK), 0)
    r64 = lax.broadcasted_iota(jnp.int32, (CHUNK, CHUNK), 0)
    c64 = lax.broadcasted_iota(jnp.int32, (CHUNK, CHUNK), 1)
    _, _, level_masks = _doubling_masks(r64, c64)
    sub_log = SUB.bit_length() - 1
    level_masks = level_masks[sub_log - 1:]
    lower = r64 > c64

    v_bf, off, o_inter = [], [], []
    for h in range(heads):
        sl = slice(h * HEAD_DIM, (h + 1) * HEAD_DIM)
        q = q_s[h]
        k = k_s[h]
        b = b_s[h]
        v = hi_ref[:, sl].astype(_BF)
        st = st_ref[h]
        b_last = b_s[h, CHUNK - 1:CHUNK, :]
        acc = jnp.zeros((CHUNK, CHUNK), _F32)
        s = SUB
        for m in level_masks:
            ref = jnp.concatenate(
                [jnp.broadcast_to(b_s[h, g + s - 1:g + s, :], (2 * s, HEAD_DIM))
                 for g in range(0, CHUNK, 2 * s)], axis=0)
            ql = q * jnp.exp2(b - ref)
            kl = k * jnp.exp2(ref - b)
            acc = jnp.where(m & lower, _dot_nt(ql, kl), acc)
            s *= 2
        off.append(acc)
        o_inter.append(_dot_nt(q * jnp.exp2(b), st))
        kd = k * jnp.exp2(b_last - b)
        st_ref[h] = st * jnp.exp2(b_last) + _dot_tn(v, kd)
        v_bf.append(v)

    attn = []
    for h in range(heads):
        rows = []
        for blk in range(CHUNK // SUB):
            lo = blk * SUB
            qb = q_s[h, lo:lo + SUB, :]
            bb = b_s[h, lo:lo + SUB, :]
            acc = jnp.zeros((SUB, CHUNK), _F32)
            for s in range(SUB):
                e = jnp.exp2(bb - b_s[h, lo + s:lo + s + 1, :])
                p = qb * e * k_s[h, lo + s:lo + s + 1, :]
                acc = jnp.where(lane == lo + s, jnp.sum(p, axis=-1, keepdims=True), acc)
            in_block = (lane >= lo) & (row + lo >= lane)
            rows.append(jnp.where(in_block, acc, off[h][lo:lo + SUB, :]))
        attn.append(jnp.concatenate(rows, axis=0).astype(_BF))

    o_intra = [_dot(attn[h], v_bf[h]) for h in range(heads)]
    for h in range(heads):
        sl = slice(h * HEAD_DIM, (h + 1) * HEAD_DIM)
        o = o_inter[h] + o_intra[h]
        ms = jnp.mean(o * o, axis=-1, keepdims=True)
        y = o * lax.rsqrt(ms + EPS) * nw * _silu(hg_ref[:, sl])
        o_ref[:, sl] = y.astype(o_ref.dtype)


def _hgrn(proj, lb, nw, *, batch, seq, col0, out_dtype=_BF):
    width = lb.shape[-1]
    heads = width // HEAD_DIM
    nc = seq // CHUNK
    cb = col0 // width

    def spec(off):
        return pl.BlockSpec((CHUNK, width), lambda bi, ci, off=off: (bi * nc + ci, cb + off))

    return pl.pallas_call(
        functools.partial(_hgrn_kernel, heads=heads),
        grid=(batch, nc),
        in_specs=[spec(0), spec(1), spec(2), spec(3),
                  pl.BlockSpec((1, width), lambda bi, ci: (0, 0)),
                  pl.BlockSpec((1, HEAD_DIM), lambda bi, ci: (0, 0))],
        out_specs=pl.BlockSpec((CHUNK, width), lambda bi, ci: (bi * nc + ci, 0)),
        out_shape=jax.ShapeDtypeStruct((batch * seq, width), out_dtype),
        scratch_shapes=[pltpu.VMEM((heads, HEAD_DIM, HEAD_DIM), _F32),
                        pltpu.VMEM((heads, CHUNK, HEAD_DIM), _F32),
                        pltpu.VMEM((heads, CHUNK, HEAD_DIM), _F32),
                        pltpu.VMEM((heads, CHUNK, HEAD_DIM), _F32)],
        compiler_params=pltpu.CompilerParams(dimension_semantics=("parallel", "arbitrary"),
                                             vmem_limit_bytes=VMEM_LIMIT),
        name="hgrn2",
    )(proj, proj, proj, proj, lb.reshape(1, width), nw.reshape(1, HEAD_DIM))


def _doubling_masks(r, c):
    eye = (r == c).astype(_F32)
    pair = (r >> 1) == (c >> 1)
    levels = []
    k = 1
    while (2 << k) <= CHUNK:
        levels.append(((r >> (k + 1)) == (c >> (k + 1))) & ((r >> k) != (c >> k)))
        k += 1
    return eye, pair, levels


def _unit_lower_inverses(mats, masks):
    eye, pair, levels = masks
    ts = [eye - jnp.where(pair, a, 0.0) for a in mats]
    for m in levels:
        xs = [_dot(jnp.where(m, a, 0.0), t) for a, t in zip(mats, ts)]
        ts = [t - _dot(t, x) for t, x in zip(ts, xs)]
    return ts


def _gdn_kernel(qkv_ref, z_ref, ba_ref, cw_ref, alog_ref, dtb_ref, nw_ref, o_ref,
                st_ref, stage_ref, *, k_heads, v_heads):
    c = pl.program_id(1)
    kw = k_heads * HEAD_DIM
    rep = v_heads // k_heads

    @pl.when(c == 0)
    def _():
        st_ref[...] = jnp.zeros_like(st_ref)
        stage_ref[0:SUBLANES, :] = jnp.zeros((SUBLANES, stage_ref.shape[1]), _F32)

    rows = qkv_ref.shape[0]
    n_chunks = rows // CHUNK
    x = qkv_ref[...]
    stage_ref[SUBLANES:SUBLANES + rows, :] = x
    cw = cw_ref[...]
    y = x * cw[GDN_CONV_WIDTH - 1:GDN_CONV_WIDTH, :]
    for j in range(GDN_CONV_WIDTH - 1):
        sh = GDN_CONV_WIDTH - 1 - j
        y = y + stage_ref[SUBLANES - sh:SUBLANES - sh + rows, :] * cw[j:j + 1, :]
    stage_ref[0:SUBLANES, :] = x[rows - SUBLANES:rows, :]
    act = _silu(y)

    r = lax.broadcasted_iota(jnp.int32, (CHUNK, CHUNK), 0)
    cc = lax.broadcasted_iota(jnp.int32, (CHUNK, CHUNK), 1)
    nw = nw_ref[...]
    masks = _doubling_masks(r, cc)
    tri = _tril_ones(CHUNK)
    hs = range(v_heads)

    qn, kn, vh, gcol, glast, qkl, amat, brow, grow = [], [], [], [], [], [], [], [], []
    for ci in range(n_chunks):
        rs = slice(ci * CHUNK, (ci + 1) * CHUNK)
        ba = ba_ref[rs, :]
        beta_all = jax.nn.sigmoid(ba)
        xs = ba + dtb_ref[...]
        softplus = jnp.maximum(xs, 0.0) + jnp.log(1.0 + jnp.exp(-jnp.abs(xs)))
        gc_all = _cumsum_rows(-jnp.exp(alog_ref[...]) * softplus, tri)
        beta_t = beta_all.T
        gc_t = gc_all.T
        q_c, k_c = [], []
        for kh in range(k_heads):
            qh = act[rs, kh * HEAD_DIM:(kh + 1) * HEAD_DIM]
            kk_ = act[rs, kw + kh * HEAD_DIM:kw + (kh + 1) * HEAD_DIM]
            qh = qh * lax.rsqrt(jnp.sum(qh * qh, axis=-1, keepdims=True) + EPS) * (HEAD_DIM ** -0.5)
            kk_ = kk_ * lax.rsqrt(jnp.sum(kk_ * kk_, axis=-1, keepdims=True) + EPS)
            q_c.append(qh.astype(_BF))
            k_c.append(kk_.astype(_BF))
        kkt = [_dot_nt(k_, k_) for k_ in k_c]
        qkt = [_dot_nt(q_, k_) for q_, k_ in zip(q_c, k_c)]
        for h in hs:
            bcol = beta_all[:, h:h + 1]
            gc = gc_all[:, v_heads + h:v_heads + h + 1]
            gr = gc_t[v_heads + h:v_heads + h + 1, :]
            dec = jnp.where(r >= cc, jnp.exp(gc - gr), 0.0)
            qn.append(q_c[h // rep])
            kn.append(k_c[h // rep])
            vh.append(act[rs, 2 * kw + h * HEAD_DIM:2 * kw + (h + 1) * HEAD_DIM].astype(_BF))
            gcol.append(gc)
            glast.append(gc[CHUNK - 1:CHUNK, :])
            brow.append(beta_t[h:h + 1, :])
            grow.append(gr)
            qkl.append((qkt[h // rep] * dec).astype(_BF))
            amat.append(jnp.where(r > cc, kkt[h // rep] * (bcol * dec), 0.0))
    n = len(amat)
    tinv = _unit_lower_inverses(amat, masks)
    tb = [tinv[i] * brow[i] for i in range(n)]
    u = [_dot(tb[i], vh[i]) for i in range(n)]
    w = [_dot(tb[i] * jnp.exp(grow[i]), kn[i]) for i in range(n)]

    st = [st_ref[h] for h in hs]
    for ci in range(n_chunks):
        rs = slice(ci * CHUNK, (ci + 1) * CHUNK)
        ix = [ci * v_heads + h for h in hs]
        st_bf = [s_.astype(_BF) for s_ in st]
        ws = [_dot(w[ix[h]], st_bf[h]) for h in hs]
        qs = [_dot(qn[ix[h]], st_bf[h]) for h in hs]
        v_new = [u[ix[h]] - ws[h] for h in hs]
        o_intra = [_dot(qkl[ix[h]], v_new[h]) for h in hs]
        kv = [_dot_tn(kn[ix[h]], v_new[h] * jnp.exp(glast[ix[h]] - gcol[ix[h]])) for h in hs]
        st = [st[h] * jnp.exp(glast[ix[h]]) + kv[h] for h in hs]
        for h in hs:
            o = jnp.exp(gcol[ix[h]]) * qs[h] + o_intra[h]
            ms = jnp.mean(o * o, axis=-1, keepdims=True)
            zh = z_ref[rs, h * HEAD_DIM:(h + 1) * HEAD_DIM]
            yh = o * lax.rsqrt(ms + EPS) * nw * _silu(zh)
            o_ref[rs, h * HEAD_DIM:(h + 1) * HEAD_DIM] = yh.astype(o_ref.dtype)
    for h in hs:
        st_ref[h] = st[h]


def _gdn(proj, ba, conv_w, alog_pad, dtb_pad, nw, *, batch, seq, qkv_col0, z_col0, k_heads, v_heads,
         out_dtype=_BF):
    qkv_w = (2 * k_heads + v_heads) * HEAD_DIM
    vw = v_heads * HEAD_DIM
    rows = GDN_STEP_CHUNKS * CHUNK
    nc = seq // rows
    qb = qkv_col0 // qkv_w
    zb = z_col0 // vw
    return pl.pallas_call(
        functools.partial(_gdn_kernel, k_heads=k_heads, v_heads=v_heads),
        grid=(batch, nc),
        in_specs=[pl.BlockSpec((rows, qkv_w), lambda bi, ci: (bi * nc + ci, qb)),
                  pl.BlockSpec((rows, vw), lambda bi, ci: (bi * nc + ci, zb)),
                  pl.BlockSpec((rows, LANES), lambda bi, ci: (bi * nc + ci, 0)),
                  pl.BlockSpec((GDN_CONV_WIDTH, qkv_w), lambda bi, ci: (0, 0)),
                  pl.BlockSpec((1, LANES), lambda bi, ci: (0, 0)),
                  pl.BlockSpec((1, LANES), lambda bi, ci: (0, 0)),
                  pl.BlockSpec((1, HEAD_DIM), lambda bi, ci: (0, 0))],
        out_specs=pl.BlockSpec((rows, vw), lambda bi, ci: (bi * nc + ci, 0)),
        out_shape=jax.ShapeDtypeStruct((batch * seq, vw), out_dtype),
        scratch_shapes=[pltpu.VMEM((v_heads, HEAD_DIM, HEAD_DIM), _F32),
                        pltpu.VMEM((SUBLANES + rows, qkv_w), _F32)],
        compiler_params=pltpu.CompilerParams(dimension_semantics=("parallel", "arbitrary"),
                                             vmem_limit_bytes=VMEM_LIMIT),
        name="gated_delta",
    )(proj, proj, ba, conv_w, alog_pad, dtb_pad, nw.reshape(1, HEAD_DIM))


def _mm2_res_kernel(a1_ref, a2_ref, w_ref, r_ref, o_ref, wbf_ref):
    @pl.when(pl.program_id(1) == 0)
    def _():
        wbf_ref[...] = w_ref[...].astype(_BF)

    k1 = a1_ref.shape[1]
    acc = jnp.dot(a1_ref[...], wbf_ref[0:k1, :], preferred_element_type=_F32)
    acc = acc + jnp.dot(a2_ref[...], wbf_ref[k1:, :], preferred_element_type=_F32)
    o_ref[...] = (acc + r_ref[...]).astype(o_ref.dtype)


def _out_proj(a1, a2, w, layer, res, *, tm, tn):
    m, k1 = a1.shape
    _, k2 = a2.shape
    n = w.shape[2]
    tm = min(tm, m)
    return pl.pallas_call(
        _mm2_res_kernel,
        grid=(n // tn, m // tm),
        in_specs=[pl.BlockSpec((tm, k1), lambda j, i: (i, 0)),
                  pl.BlockSpec((tm, k2), lambda j, i: (i, 0)),
                  pl.BlockSpec((None, k1 + k2, tn), lambda j, i: (layer, 0, j)),
                  pl.BlockSpec((tm, tn), lambda j, i: (i, j))],
        out_specs=pl.BlockSpec((tm, tn), lambda j, i: (i, j)),
        out_shape=jax.ShapeDtypeStruct((m, n), _F32),
        scratch_shapes=[pltpu.VMEM((k1 + k2, tn), _BF)],
        compiler_params=pltpu.CompilerParams(dimension_semantics=("parallel", "arbitrary"),
                                             vmem_limit_bytes=VMEM_LIMIT),
        name="out_proj",
    )(a1, a2, w, res)


def _ffn_in_kernel(a_ref, wg_ref, wu_ref, cw_ref, cb_ref, o_ref, wg_bf, wu_bf, stage_ref, *, seq_tiles):
    i = pl.program_id(1)
    rows = a_ref.shape[0]

    @pl.when(i == 0)
    def _():
        wg_bf[...] = wg_ref[...].astype(_BF)
        wu_bf[...] = wu_ref[...].astype(_BF)

    @pl.when(i % seq_tiles == 0)
    def _():
        stage_ref[0:SUBLANES, :] = jnp.zeros((SUBLANES, stage_ref.shape[1]), _F32)

    a = a_ref[...]
    g = jnp.dot(a, wg_bf[...], preferred_element_type=_F32)
    u = jnp.dot(a, wu_bf[...], preferred_element_type=_F32)
    stage_ref[SUBLANES:SUBLANES + rows, :] = g
    cw = cw_ref[...]
    y = g * cw[FFN_CONV_WIDTH - 1:FFN_CONV_WIDTH, :] + cb_ref[...]
    for j in range(FFN_CONV_WIDTH - 1):
        sh = FFN_CONV_WIDTH - 1 - j
        y = y + stage_ref[SUBLANES - sh:SUBLANES - sh + rows, :] * cw[j:j + 1, :]
    stage_ref[0:SUBLANES, :] = g[rows - SUBLANES:rows, :]
    o_ref[...] = (_silu(y) * u).astype(o_ref.dtype)


def _ffn_in(a, w, layer, cw, cb, *, seq, tm, tn):
    m, k = a.shape
    d_ff = cb.shape[-1]
    tm = min(tm, seq)
    ncol = d_ff // tn
    return pl.pallas_call(
        functools.partial(_ffn_in_kernel, seq_tiles=seq // tm),
        grid=(ncol, m // tm),
        in_specs=[pl.BlockSpec((tm, k), lambda j, i: (i, 0)),
                  pl.BlockSpec((None, k, tn), lambda j, i: (layer, 0, j)),
                  pl.BlockSpec((None, k, tn), lambda j, i: (layer, 0, ncol + j)),
                  pl.BlockSpec((FFN_CONV_WIDTH, tn), lambda j, i: (0, j)),
                  pl.BlockSpec((1, tn), lambda j, i: (0, j))],
        out_specs=pl.BlockSpec((tm, tn), lambda j, i: (i, j)),
        out_shape=jax.ShapeDtypeStruct((m, d_ff), _BF),
        scratch_shapes=[pltpu.VMEM((k, tn), _BF), pltpu.VMEM((k, tn), _BF),
                        pltpu.VMEM((SUBLANES + tm, tn), _F32)],
        compiler_params=pltpu.CompilerParams(dimension_semantics=("parallel", "arbitrary"),
                                             vmem_limit_bytes=VMEM_LIMIT),
        name="ffn_in",
    )(a, w, w, cw, cb.reshape(1, d_ff))


def kernel(x, norm_mix_w, w_in, hgrn_lb_raw, hgrn_norm_w, gdn_conv_w, gdn_A_log, gdn_dt_bias, gdn_norm_w,
           w_out, norm_ffn_w, w_ffn_in, ffn_conv_w, ffn_conv_b, w_ffn_out, final_norm_w):
    batch, seq, d = x.shape
    depth = w_in.shape[0]
    v_heads = gdn_A_log.shape[1]
    gdn_w = v_heads * HEAD_DIM
    k_heads = (gdn_conv_w.shape[2] - gdn_w) // (2 * HEAD_DIM)
    qkv_w = gdn_conv_w.shape[2]
    hgrn_w = hgrn_lb_raw.shape[1]
    d_ff = ffn_conv_b.shape[1]
    main_w = 4 * hgrn_w + qkv_w + gdn_w
    t = batch * seq

    lb_all = jnp.cumsum(jax.nn.softmax(hgrn_lb_raw.astype(_F32), axis=0), axis=0)

    w_in_t = jnp.swapaxes(w_in, 1, 2)

    xt = x.reshape(t, d)
    for l in range(depth):
        w_gate = jnp.pad(w_in[l, :, main_w:], ((0, 0), (0, LANES - 2 * v_heads))).astype(_BF)
        pad_lo = jnp.zeros((v_heads,), _F32)
        pad_hi = jnp.zeros((LANES - 2 * v_heads,), _F32)
        alog_pad = jnp.concatenate([pad_lo, gdn_A_log[l].astype(_F32), pad_hi]).reshape(1, LANES)
        dtb_pad = jnp.concatenate([pad_lo, gdn_dt_bias[l].astype(_F32), pad_hi]).reshape(1, LANES)

        h = _rmsnorm(xt, norm_mix_w[l], _BF)
        tm, tn = TILES["in_proj"]
        proj = _matmul_wcast(h, w_in_t, l, n_cols=main_w, tm=tm, tn=tn, out_dtype=_F32, name="in_proj",
                             transposed=True)
        tm, tn = TILES["in_proj_gates"]
        ba = _matmul(h, w_gate, tm=tm, tn=tn, out_dtype=_F32, name="in_proj_gates")

        o_h = _hgrn(proj, lb_all[l], hgrn_norm_w[l], batch=batch, seq=seq, col0=0)
        o_g = _gdn(proj, ba, gdn_conv_w[l], alog_pad, dtb_pad, gdn_norm_w[l], batch=batch, seq=seq,
                   qkv_col0=4 * hgrn_w, z_col0=4 * hgrn_w + qkv_w, k_heads=k_heads, v_heads=v_heads)
        tm, tn = TILES["out_proj"]
        xt = _out_proj(o_h, o_g, w_out, l, xt, tm=tm, tn=tn)

        h2 = _rmsnorm(xt, norm_ffn_w[l], _BF)
        tm, tn = TILES["ffn_in"]
        act = _ffn_in(h2, w_ffn_in, l, ffn_conv_w[l], ffn_conv_b[l], seq=seq, tm=tm, tn=tn)
        tm, tn = TILES["ffn_out"]
        xt = _matmul(act, w_ffn_out[l].astype(_BF), tm=tm, tn=tn, out_dtype=_F32, res=xt, name="ffn_out")

    out = _rmsnorm(xt, final_norm_w, x.dtype)
    return out.reshape(batch, seq, d)
```

```python
import functools

import jax
import jax.numpy as jnp
from jax import lax
from jax.experimental import pallas as pl
from jax.experimental.pallas import tpu as pltpu

HEAD_DIM = 128
CHUNK = 64
SUB = 8
LOG2E = 1.4426950408889634
HGRN_STEP_CHUNKS = 2
GDN_STEP_CHUNKS = 2
GDN_CONV_WIDTH = 4
FFN_CONV_WIDTH = 3
EPS = 1e-6
LANES = 128
SUBLANES = 8
VMEM_LIMIT = 56 * 1024 * 1024

TILES = {
    "in_proj": (512, 1024),
    "out_proj": (1024, 512),
    "ffn_in": (1024, 256),
    "ffn_out": (512, 512),
}

_BF = jnp.bfloat16
_F32 = jnp.float32


def _dot(a, b):
    return jnp.dot(a.astype(_BF), b.astype(_BF), preferred_element_type=_F32)


def _dot_nt(a, b):
    return lax.dot_general(a.astype(_BF), b.astype(_BF), (((1,), (1,)), ((), ())),
                           preferred_element_type=_F32)


def _dot_tn(a, b):
    return lax.dot_general(a.astype(_BF), b.astype(_BF), (((0,), (0,)), ((), ())),
                           preferred_element_type=_F32)


def _cumsum_rows(x, tri_bf):
    hi = x.astype(_BF)
    r1 = x - hi.astype(_F32)
    mid = r1.astype(_BF)
    lo = (r1 - mid.astype(_F32)).astype(_BF)
    acc = jnp.dot(tri_bf, hi, preferred_element_type=_F32)
    acc = acc + jnp.dot(tri_bf, mid, preferred_element_type=_F32)
    return acc + jnp.dot(tri_bf, lo, preferred_element_type=_F32)


def _tril_ones(n):
    r = lax.broadcasted_iota(jnp.int32, (n, n), 0)
    c = lax.broadcasted_iota(jnp.int32, (n, n), 1)
    return (r >= c).astype(_BF)


def _silu(x):
    return x * jax.nn.sigmoid(x)


def _rmsnorm_kernel(x_ref, w_ref, o_ref):
    x = x_ref[...]
    ms = jnp.mean(x * x, axis=-1, keepdims=True)
    o_ref[...] = (x * lax.rsqrt(ms + EPS) * w_ref[...]).astype(o_ref.dtype)


def _rmsnorm(x, w, out_dtype, tm=512):
    t, d = x.shape
    return pl.pallas_call(
        _rmsnorm_kernel,
        grid=(t // tm,),
        in_specs=[pl.BlockSpec((tm, d), lambda i: (i, 0)), pl.BlockSpec((1, d), lambda i: (0, 0))],
        out_specs=pl.BlockSpec((tm, d), lambda i: (i, 0)),
        out_shape=jax.ShapeDtypeStruct((t, d), out_dtype),
        compiler_params=pltpu.CompilerParams(dimension_semantics=("parallel",),
                                             vmem_limit_bytes=VMEM_LIMIT),
        name="rmsnorm",
    )(x, w.reshape(1, d))


def _rmsnorm_proj_kernel(x_ref, w_ref, p_ref, o_ref, y_ref):
    x = x_ref[...]
    ms = jnp.mean(x * x, axis=-1, keepdims=True)
    h = (x * lax.rsqrt(ms + EPS) * w_ref[...]).astype(o_ref.dtype)
    o_ref[...] = h
    y_ref[...] = jnp.dot(h, p_ref[...], preferred_element_type=_F32)


def _rmsnorm_proj(x, w, p, out_dtype, tm=512):
    t, d = x.shape
    tm = min(tm, t)
    return pl.pallas_call(
        _rmsnorm_proj_kernel,
        grid=(t // tm,),
        in_specs=[pl.BlockSpec((tm, d), lambda i: (i, 0)), pl.BlockSpec((1, d), lambda i: (0, 0)),
                  pl.BlockSpec((d, LANES), lambda i: (0, 0))],
        out_specs=[pl.BlockSpec((tm, d), lambda i: (i, 0)), pl.BlockSpec((tm, LANES), lambda i: (i, 0))],
        out_shape=[jax.ShapeDtypeStruct((t, d), out_dtype), jax.ShapeDtypeStruct((t, LANES), _F32)],
        compiler_params=pltpu.CompilerParams(dimension_semantics=("parallel",),
                                             vmem_limit_bytes=VMEM_LIMIT),
        name="rmsnorm_gates",
    )(x, w.reshape(1, d), p)


def _mm_kernel(a_ref, b_ref, o_ref):
    o_ref[...] = jnp.dot(a_ref[...], b_ref[...], preferred_element_type=_F32).astype(o_ref.dtype)


def _mm_res_kernel(a_ref, b_ref, r_ref, o_ref):
    acc = jnp.dot(a_ref[...], b_ref[...], preferred_element_type=_F32)
    o_ref[...] = (acc + r_ref[...]).astype(o_ref.dtype)


def _matmul(a, b, *, tm, tn, out_dtype, res=None, name="matmul"):
    m, k = a.shape
    _, n = b.shape
    tm = min(tm, m)
    in_specs = [pl.BlockSpec((tm, k), lambda i, j: (i, 0)), pl.BlockSpec((k, tn), lambda i, j: (0, j))]
    args = [a, b]
    kern = _mm_kernel
    if res is not None:
        in_specs.append(pl.BlockSpec((tm, tn), lambda i, j: (i, j)))
        args.append(res)
        kern = _mm_res_kernel
    return pl.pallas_call(
        kern,
        grid=(m // tm, n // tn),
        in_specs=in_specs,
        out_specs=pl.BlockSpec((tm, tn), lambda i, j: (i, j)),
        out_shape=jax.ShapeDtypeStruct((m, n), out_dtype),
        compiler_params=pltpu.CompilerParams(dimension_semantics=("parallel", "arbitrary"),
                                             vmem_limit_bytes=VMEM_LIMIT),
        name=name,
    )(*args)


def _mm_wcast_kernel(a_ref, w_ref, o_ref, wbf_ref):
    @pl.when(pl.program_id(1) == 0)
    def _():
        wbf_ref[...] = w_ref[...].astype(_BF)

    o_ref[...] = jnp.dot(a_ref[...], wbf_ref[...], preferred_element_type=_F32).astype(o_ref.dtype)


def _mm_wcast_t_kernel(a_ref, wt_ref, o_ref, wbf_ref):
    @pl.when(pl.program_id(1) == 0)
    def _():
        wbf_ref[...] = wt_ref[...].astype(_BF)

    o_ref[...] = lax.dot_general(a_ref[...], wbf_ref[...], (((1,), (1,)), ((), ())),
                                 preferred_element_type=_F32).astype(o_ref.dtype)


def _matmul_wcast(a, w, layer, *, n_cols, tm, tn, out_dtype, name, transposed=False):
    m, k = a.shape
    tm = min(tm, m)
    if transposed:
        kern, w_spec, w_scr = (_mm_wcast_t_kernel, pl.BlockSpec((None, tn, k), lambda j, i: (layer, j, 0)),
                               pltpu.VMEM((tn, k), _BF))
    else:
        kern, w_spec, w_scr = (_mm_wcast_kernel, pl.BlockSpec((None, k, tn), lambda j, i: (layer, 0, j)),
                               pltpu.VMEM((k, tn), _BF))
    return pl.pallas_call(
        kern,
        grid=(n_cols // tn, m // tm),
        in_specs=[pl.BlockSpec((tm, k), lambda j, i: (i, 0)), w_spec],
        out_specs=pl.BlockSpec((tm, tn), lambda j, i: (i, j)),
        out_shape=jax.ShapeDtypeStruct((m, n_cols), out_dtype),
        scratch_shapes=[w_scr],
        compiler_params=pltpu.CompilerParams(dimension_semantics=("parallel", "arbitrary"),
                                             vmem_limit_bytes=VMEM_LIMIT),
        name=name,
    )(a, w)


def _hgrn_kernel(hq_ref, hf_ref, hi_ref, hg_ref, lb_ref, nw_ref, o_ref,
                 st_ref, q_s, k_s, b_s, *, heads):
    @pl.when(pl.program_id(1) == 0)
    def _():
        st_ref[...] = jnp.zeros_like(st_ref)

    for ci in range(hq_ref.shape[0] // CHUNK):
        rs = pl.ds(ci * CHUNK, CHUNK)
        _hgrn_chunk(hq_ref.at[rs, :], hf_ref.at[rs, :], hi_ref.at[rs, :], hg_ref.at[rs, :], lb_ref, nw_ref,
                    o_ref.at[rs, :], st_ref, q_s, k_s, b_s, heads=heads)


def _hgrn_chunk(hq_ref, hf_ref, hi_ref, hg_ref, lb_ref, nw_ref, o_ref,
                st_ref, q_s, k_s, b_s, *, heads):
    lb = lb_ref[...]
    f = lb + (1.0 - lb) * jax.nn.sigmoid(hf_ref[...])
    b_all = _cumsum_rows(jnp.log(f) * LOG2E, _tril_ones(CHUNK))
    q_all = _silu(hq_ref[...]) * (HEAD_DIM ** -0.5)
    k_all = 1.0 - f
    for h in range(heads):
        sl = slice(h * HEAD_DIM, (h + 1) * HEAD_DIM)
        q_s[h] = q_all[:, sl]
        k_s[h] = k_all[:, sl]
        b_s[h] = b_all[:, sl]

    nw = nw_ref[...]
    lane = lax.broadcasted_iota(jnp.int32, (SUB, CHUNK), 1)
    row = lax.broadcasted_iota(jnp.int32, (SUB, CHUNK), 0)
    r64 = lax.broadcasted_iota(jnp.int32, (CHUNK, CHUNK), 0)
    c64 = lax.broadcasted_iota(jnp.int32, (CHUNK, CHUNK), 1)
    _, _, level_masks = _doubling_masks(r64, c64)
    sub_log = SUB.bit_length() - 1
    level_masks = level_masks[sub_log - 1:]
    lower = r64 > c64

    v_bf, off, o_inter = [], [], []
    for h in range(heads):
        sl = slice(h * HEAD_DIM, (h + 1) * HEAD_DIM)
        q = q_s[h]
        k = k_s[h]
        b = b_s[h]
        v = hi_ref[:, sl].astype(_BF)
        st = st_ref[h]
        b_last = b_s[h, CHUNK - 1:CHUNK, :]
        acc = jnp.zeros((CHUNK, CHUNK), _F32)
        s = SUB
        for m in level_masks:
            ref = jnp.concatenate(
                [jnp.broadcast_to(b_s[h, g + s - 1:g + s, :], (2 * s, HEAD_DIM))
                 for g in range(0, CHUNK, 2 * s)], axis=0)
            ql = q * jnp.exp2(b - ref)
            kl = k * jnp.exp2(ref - b)
            acc = jnp.where(m & lower, _dot_nt(ql, kl), acc)
            s *= 2
        off.append(acc)
        o_inter.append(_dot_nt(q * jnp.exp2(b), st))
        kd = k * jnp.exp2(b_last - b)
        st_ref[h] = st * jnp.exp2(b_last) + _dot_tn(v, kd)
        v_bf.append(v)

    attn = []
    for h in range(heads):
        rows = []
        for blk in range(CHUNK // SUB):
            lo = blk * SUB
            qb = q_s[h, lo:lo + SUB, :]
            bb = b_s[h, lo:lo + SUB, :]
            acc = jnp.zeros((SUB, CHUNK), _F32)
            for s in range(SUB):
                e = jnp.exp2(bb - b_s[h, lo + s:lo + s + 1, :])
                p = qb * e * k_s[h, lo + s:lo + s + 1, :]
                acc = jnp.where(lane == lo + s, jnp.sum(p, axis=-1, keepdims=True), acc)
            in_block = (lane >= lo) & (row + lo >= lane)
            rows.append(jnp.where(in_block, acc, off[h][lo:lo + SUB, :]))
        attn.append(jnp.concatenate(rows, axis=0).astype(_BF))

    o_intra = [_dot(attn[h], v_bf[h]) for h in range(heads)]
    for h in range(heads):
        sl = slice(h * HEAD_DIM, (h + 1) * HEAD_DIM)
        o = o_inter[h] + o_intra[h]
        ms = jnp.mean(o * o, axis=-1, keepdims=True)
        y = o * lax.rsqrt(ms + EPS) * nw * _silu(hg_ref[:, sl])
        o_ref[:, sl] = y.astype(o_ref.dtype)


def _hgrn(proj, lb, nw, *, batch, seq, col0, out_dtype=_BF):
    width = lb.shape[-1]
    heads = width // HEAD_DIM
    rows = HGRN_STEP_CHUNKS * CHUNK
    nc = seq // rows
    cb = col0 // width

    def spec(off):
        return pl.BlockSpec((rows, width), lambda bi, ci, off=off: (bi * nc + ci, cb + off))

    return pl.pallas_call(
        functools.partial(_hgrn_kernel, heads=heads),
        grid=(batch, nc),
        in_specs=[spec(0), spec(1), spec(2), spec(3),
                  pl.BlockSpec((1, width), lambda bi, ci: (0, 0)),
                  pl.BlockSpec((1, HEAD_DIM), lambda bi, ci: (0, 0))],
        out_specs=pl.BlockSpec((rows, width), lambda bi, ci: (bi * nc + ci, 0)),
        out_shape=jax.ShapeDtypeStruct((batch * seq, width), out_dtype),
        scratch_shapes=[pltpu.VMEM((heads, HEAD_DIM, HEAD_DIM), _F32),
                        pltpu.VMEM((heads, CHUNK, HEAD_DIM), _F32),
                        pltpu.VMEM((heads, CHUNK, HEAD_DIM), _F32),
                        pltpu.VMEM((heads, CHUNK, HEAD_DIM), _F32)],
        compiler_params=pltpu.CompilerParams(dimension_semantics=("parallel", "arbitrary"),
                                             vmem_limit_bytes=VMEM_LIMIT),
        name="hgrn2",
    )(proj, proj, proj, proj, lb.reshape(1, width), nw.reshape(1, HEAD_DIM))


def _doubling_masks(r, c):
    eye = (r == c).astype(_F32)
    pair = (r >> 1) == (c >> 1)
    levels = []
    k = 1
    while (2 << k) <= CHUNK:
        levels.append(((r >> (k + 1)) == (c >> (k + 1))) & ((r >> k) != (c >> k)))
        k += 1
    return eye, pair, levels


def _unit_lower_inverses(mats, masks):
    eye, pair, levels = masks
    ts = [eye - jnp.where(pair, a, 0.0) for a in mats]
    for m in levels:
        xs = [_dot(jnp.where(m, a, 0.0), t) for a, t in zip(mats, ts)]
        ts = [t - _dot(t, x) for t, x in zip(ts, xs)]
    return ts


def _gdn_kernel(qkv_ref, z_ref, ba_ref, cw_ref, alog_ref, dtb_ref, nw_ref, o_ref,
                st_ref, stage_ref, *, k_heads, v_heads):
    c = pl.program_id(1)
    kw = k_heads * HEAD_DIM
    rep = v_heads // k_heads

    @pl.when(c == 0)
    def _():
        st_ref[...] = jnp.zeros_like(st_ref)
        stage_ref[0:SUBLANES, :] = jnp.zeros((SUBLANES, stage_ref.shape[1]), _F32)

    rows = qkv_ref.shape[0]
    n_chunks = rows // CHUNK
    x = qkv_ref[...]
    stage_ref[SUBLANES:SUBLANES + rows, :] = x
    cw = cw_ref[...]
    y = x * cw[GDN_CONV_WIDTH - 1:GDN_CONV_WIDTH, :]
    for j in range(GDN_CONV_WIDTH - 1):
        sh = GDN_CONV_WIDTH - 1 - j
        y = y + stage_ref[SUBLANES - sh:SUBLANES - sh + rows, :] * cw[j:j + 1, :]
    stage_ref[0:SUBLANES, :] = x[rows - SUBLANES:rows, :]
    act = _silu(y)

    r = lax.broadcasted_iota(jnp.int32, (CHUNK, CHUNK), 0)
    cc = lax.broadcasted_iota(jnp.int32, (CHUNK, CHUNK), 1)
    nw = nw_ref[...]
    masks = _doubling_masks(r, cc)
    tri = _tril_ones(CHUNK)
    hs = range(v_heads)

    qn, kn, vh, gcol, glast, qkl, amat, brow, grow = [], [], [], [], [], [], [], [], []
    for ci in range(n_chunks):
        rs = slice(ci * CHUNK, (ci + 1) * CHUNK)
        ba = ba_ref[rs, :]
        beta_all = jax.nn.sigmoid(ba)
        xs = ba + dtb_ref[...]
        softplus = jnp.maximum(xs, 0.0) + jnp.log(1.0 + jnp.exp(-jnp.abs(xs)))
        gc_all = _cumsum_rows(-jnp.exp(alog_ref[...]) * softplus, tri)
        beta_t = beta_all.T
        gc_t = gc_all.T
        q_c, k_c = [], []
        for kh in range(k_heads):
            qh = act[rs, kh * HEAD_DIM:(kh + 1) * HEAD_DIM]
            kk_ = act[rs, kw + kh * HEAD_DIM:kw + (kh + 1) * HEAD_DIM]
            qh = qh * lax.rsqrt(jnp.sum(qh * qh, axis=-1, keepdims=True) + EPS) * (HEAD_DIM ** -0.5)
            kk_ = kk_ * lax.rsqrt(jnp.sum(kk_ * kk_, axis=-1, keepdims=True) + EPS)
            q_c.append(qh.astype(_BF))
            k_c.append(kk_.astype(_BF))
        kkt = [_dot_nt(k_, k_) for k_ in k_c]
        qkt = [_dot_nt(q_, k_) for q_, k_ in zip(q_c, k_c)]
        for h in hs:
            bcol = beta_all[:, h:h + 1]
            gc = gc_all[:, v_heads + h:v_heads + h + 1]
            gr = gc_t[v_heads + h:v_heads + h + 1, :]
            dec = jnp.where(r >= cc, jnp.exp(gc - gr), 0.0)
            qn.append(q_c[h // rep])
            kn.append(k_c[h // rep])
            vh.append(act[rs, 2 * kw + h * HEAD_DIM:2 * kw + (h + 1) * HEAD_DIM].astype(_BF))
            gcol.append(gc)
            glast.append(gc[CHUNK - 1:CHUNK, :])
            brow.append(beta_t[h:h + 1, :])
            grow.append(gr)
            qkl.append((qkt[h // rep] * dec).astype(_BF))
            amat.append(jnp.where(r > cc, kkt[h // rep] * (bcol * dec), 0.0))
    n = len(amat)
    tinv = _unit_lower_inverses(amat, masks)
    tb = [tinv[i] * brow[i] for i in range(n)]
    u = [_dot(tb[i], vh[i]) for i in range(n)]
    w = [_dot(tb[i] * jnp.exp(grow[i]), kn[i]) for i in range(n)]

    st = [st_ref[h] for h in hs]
    for ci in range(n_chunks):
        rs = slice(ci * CHUNK, (ci + 1) * CHUNK)
        ix = [ci * v_heads + h for h in hs]
        st_bf = [s_.astype(_BF) for s_ in st]
        ws = [_dot(w[ix[h]], st_bf[h]) for h in hs]
        qs = [_dot(qn[ix[h]], st_bf[h]) for h in hs]
        v_new = [u[ix[h]] - ws[h] for h in hs]
        o_intra = [_dot(qkl[ix[h]], v_new[h]) for h in hs]
        kv = [_dot_tn(kn[ix[h]], v_new[h] * jnp.exp(glast[ix[h]] - gcol[ix[h]])) for h in hs]
        st = [st[h] * jnp.exp(glast[ix[h]]) + kv[h] for h in hs]
        for h in hs:
            o = jnp.exp(gcol[ix[h]]) * qs[h] + o_intra[h]
            ms = jnp.mean(o * o, axis=-1, keepdims=True)
            zh = z_ref[rs, h * HEAD_DIM:(h + 1) * HEAD_DIM]
            yh = o * lax.rsqrt(ms + EPS) * nw * _silu(zh)
            o_ref[rs, h * HEAD_DIM:(h + 1) * HEAD_DIM] = yh.astype(o_ref.dtype)
    for h in hs:
        st_ref[h] = st[h]


def _gdn(proj, ba, conv_w, alog_pad, dtb_pad, nw, *, batch, seq, qkv_col0, z_col0, k_heads, v_heads,
         out_dtype=_BF):
    qkv_w = (2 * k_heads + v_heads) * HEAD_DIM
    vw = v_heads * HEAD_DIM
    rows = GDN_STEP_CHUNKS * CHUNK
    nc = seq // rows
    qb = qkv_col0 // qkv_w
    zb = z_col0 // vw
    return pl.pallas_call(
        functools.partial(_gdn_kernel, k_heads=k_heads, v_heads=v_heads),
        grid=(batch, nc),
        in_specs=[pl.BlockSpec((rows, qkv_w), lambda bi, ci: (bi * nc + ci, qb)),
                  pl.BlockSpec((rows, vw), lambda bi, ci: (bi * nc + ci, zb)),
                  pl.BlockSpec((rows, LANES), lambda bi, ci: (bi * nc + ci, 0)),
                  pl.BlockSpec((GDN_CONV_WIDTH, qkv_w), lambda bi, ci: (0, 0)),
                  pl.BlockSpec((1, LANES), lambda bi, ci: (0, 0)),
                  pl.BlockSpec((1, LANES), lambda bi, ci: (0, 0)),
                  pl.BlockSpec((1, HEAD_DIM), lambda bi, ci: (0, 0))],
        out_specs=pl.BlockSpec((rows, vw), lambda bi, ci: (bi * nc + ci, 0)),
        out_shape=jax.ShapeDtypeStruct((batch * seq, vw), out_dtype),
        scratch_shapes=[pltpu.VMEM((v_heads, HEAD_DIM, HEAD_DIM), _F32),
                        pltpu.VMEM((SUBLANES + rows, qkv_w), _F32)],
        compiler_params=pltpu.CompilerParams(dimension_semantics=("parallel", "arbitrary"),
                                             vmem_limit_bytes=VMEM_LIMIT),
        name="gated_delta",
    )(proj, proj, ba, conv_w, alog_pad, dtb_pad, nw.reshape(1, HEAD_DIM))


def _mm2_res_kernel(a1_ref, a2_ref, w_ref, r_ref, o_ref, wbf_ref):
    @pl.when(pl.program_id(1) == 0)
    def _():
        wbf_ref[...] = w_ref[...].astype(_BF)

    k1 = a1_ref.shape[1]
    acc = jnp.dot(a1_ref[...], wbf_ref[0:k1, :], preferred_element_type=_F32)
    acc = acc + jnp.dot(a2_ref[...], wbf_ref[k1:, :], preferred_element_type=_F32)
    o_ref[...] = (acc + r_ref[...]).astype(o_ref.dtype)


def _out_proj(a1, a2, w, layer, res, *, tm, tn):
    m, k1 = a1.shape
    _, k2 = a2.shape
    n = w.shape[2]
    tm = min(tm, m)
    return pl.pallas_call(
        _mm2_res_kernel,
        grid=(n // tn, m // tm),
        in_specs=[pl.BlockSpec((tm, k1), lambda j, i: (i, 0)),
                  pl.BlockSpec((tm, k2), lambda j, i: (i, 0)),
                  pl.BlockSpec((None, k1 + k2, tn), lambda j, i: (layer, 0, j)),
                  pl.BlockSpec((tm, tn), lambda j, i: (i, j))],
        out_specs=pl.BlockSpec((tm, tn), lambda j, i: (i, j)),
        out_shape=jax.ShapeDtypeStruct((m, n), _F32),
        scratch_shapes=[pltpu.VMEM((k1 + k2, tn), _BF)],
        compiler_params=pltpu.CompilerParams(dimension_semantics=("parallel", "arbitrary"),
                                             vmem_limit_bytes=VMEM_LIMIT),
        name="out_proj",
    )(a1, a2, w, res)


def _ffn_in_kernel(a_ref, wg_ref, wu_ref, cw_ref, cb_ref, o_ref, wg_bf, wu_bf, stage_ref, *, seq_tiles):
    i = pl.program_id(1)
    rows = a_ref.shape[0]

    @pl.when(i == 0)
    def _():
        wg_bf[...] = wg_ref[...].astype(_BF)
        wu_bf[...] = wu_ref[...].astype(_BF)

    @pl.when(i % seq_tiles == 0)
    def _():
        stage_ref[0:SUBLANES, :] = jnp.zeros((SUBLANES, stage_ref.shape[1]), _F32)

    a = a_ref[...]
    g = jnp.dot(a, wg_bf[...], preferred_element_type=_F32)
    u = jnp.dot(a, wu_bf[...], preferred_element_type=_F32)
    stage_ref[SUBLANES:SUBLANES + rows, :] = g
    cw = cw_ref[...]
    y = g * cw[FFN_CONV_WIDTH - 1:FFN_CONV_WIDTH, :] + cb_ref[...]
    for j in range(FFN_CONV_WIDTH - 1):
        sh = FFN_CONV_WIDTH - 1 - j
        y = y + stage_ref[SUBLANES - sh:SUBLANES - sh + rows, :] * cw[j:j + 1, :]
    stage_ref[0:SUBLANES, :] = g[rows - SUBLANES:rows, :]
    o_ref[...] = (_silu(y) * u).astype(o_ref.dtype)


def _ffn_in(a, w, layer, cw, cb, *, seq, tm, tn):
    m, k = a.shape
    d_ff = cb.shape[-1]
    tm = min(tm, seq)
    ncol = d_ff // tn
    return pl.pallas_call(
        functools.partial(_ffn_in_kernel, seq_tiles=seq // tm),
        grid=(ncol, m // tm),
        in_specs=[pl.BlockSpec((tm, k), lambda j, i: (i, 0)),
                  pl.BlockSpec((None, k, tn), lambda j, i: (layer, 0, j)),
                  pl.BlockSpec((None, k, tn), lambda j, i: (layer, 0, ncol + j)),
                  pl.BlockSpec((FFN_CONV_WIDTH, tn), lambda j, i: (0, j)),
                  pl.BlockSpec((1, tn), lambda j, i: (0, j))],
        out_specs=pl.BlockSpec((tm, tn), lambda j, i: (i, j)),
        out_shape=jax.ShapeDtypeStruct((m, d_ff), _BF),
        scratch_shapes=[pltpu.VMEM((k, tn), _BF), pltpu.VMEM((k, tn), _BF),
                        pltpu.VMEM((SUBLANES + tm, tn), _F32)],
        compiler_params=pltpu.CompilerParams(dimension_semantics=("parallel", "arbitrary"),
                                             vmem_limit_bytes=VMEM_LIMIT),
        name="ffn_in",
    )(a, w, w, cw, cb.reshape(1, d_ff))


def kernel(x, norm_mix_w, w_in, hgrn_lb_raw, hgrn_norm_w, gdn_conv_w, gdn_A_log, gdn_dt_bias, gdn_norm_w,
           w_out, norm_ffn_w, w_ffn_in, ffn_conv_w, ffn_conv_b, w_ffn_out, final_norm_w):
    batch, seq, d = x.shape
    depth = w_in.shape[0]
    v_heads = gdn_A_log.shape[1]
    gdn_w = v_heads * HEAD_DIM
    k_heads = (gdn_conv_w.shape[2] - gdn_w) // (2 * HEAD_DIM)
    qkv_w = gdn_conv_w.shape[2]
    hgrn_w = hgrn_lb_raw.shape[1]
    d_ff = ffn_conv_b.shape[1]
    main_w = 4 * hgrn_w + qkv_w + gdn_w
    t = batch * seq

    lb_all = jnp.cumsum(jax.nn.softmax(hgrn_lb_raw.astype(_F32), axis=0), axis=0)

    w_in_t = jnp.swapaxes(w_in, 1, 2)

    xt = x.reshape(t, d)
    for l in range(depth):
        w_gate = jnp.pad(w_in[l, :, main_w:], ((0, 0), (0, LANES - 2 * v_heads))).astype(_BF)
        pad_lo = jnp.zeros((v_heads,), _F32)
        pad_hi = jnp.zeros((LANES - 2 * v_heads,), _F32)
        alog_pad = jnp.concatenate([pad_lo, gdn_A_log[l].astype(_F32), pad_hi]).reshape(1, LANES)
        dtb_pad = jnp.concatenate([pad_lo, gdn_dt_bias[l].astype(_F32), pad_hi]).reshape(1, LANES)

        h, ba = _rmsnorm_proj(xt, norm_mix_w[l], w_gate, _BF)
        tm, tn = TILES["in_proj"]
        proj = _matmul_wcast(h, w_in_t, l, n_cols=main_w, tm=tm, tn=tn, out_dtype=_F32, name="in_proj",
                             transposed=True)

        o_h = _hgrn(proj, lb_all[l], hgrn_norm_w[l], batch=batch, seq=seq, col0=0)
        o_g = _gdn(proj, ba, gdn_conv_w[l], alog_pad, dtb_pad, gdn_norm_w[l], batch=batch, seq=seq,
                   qkv_col0=4 * hgrn_w, z_col0=4 * hgrn_w + qkv_w, k_heads=k_heads, v_heads=v_heads)
        tm, tn = TILES["out_proj"]
        xt = _out_proj(o_h, o_g, w_out, l, xt, tm=tm, tn=tn)

        h2 = _rmsnorm(xt, norm_ffn_w[l], _BF)
        tm, tn = TILES["ffn_in"]
        act = _ffn_in(h2, w_ffn_in, l, ffn_conv_w[l], ffn_conv_b[l], seq=seq, tm=tm, tn=tn)
        tm, tn = TILES["ffn_out"]
        xt = _matmul(act, w_ffn_out[l].astype(_BF), tm=tm, tn=tn, out_dtype=_F32, res=xt, name="ffn_out")

    out = _rmsnorm(xt, final_norm_w, x.dtype)
    return out.reshape(batch, seq, d)
```

```python
import functools

import jax
import jax.numpy as jnp
from jax import lax
from jax.experimental import pallas as pl
from jax.experimental.pallas import tpu as pltpu

HEAD_DIM = 128
CHUNK = 64
SUB = 8
LOG2E = 1.4426950408889634
HGRN_STEP_CHUNKS = 4
GDN_STEP_CHUNKS = 2
GDN_CONV_WIDTH = 4
FFN_CONV_WIDTH = 3
EPS = 1e-6
LANES = 128
SUBLANES = 8
VMEM_LIMIT = 56 * 1024 * 1024

TILES = {
    "in_proj": (512, 1024),
    "out_proj": (1024, 512),
    "ffn_in": (1024, 256),
    "ffn_out": (512, 512),
}

_BF = jnp.bfloat16
_F32 = jnp.float32


def _dot(a, b):
    return jnp.dot(a.astype(_BF), b.astype(_BF), preferred_element_type=_F32)


def _dot_nt(a, b):
    return lax.dot_general(a.astype(_BF), b.astype(_BF), (((1,), (1,)), ((), ())),
                           preferred_element_type=_F32)


def _dot_tn(a, b):
    return lax.dot_general(a.astype(_BF), b.astype(_BF), (((0,), (0,)), ((), ())),
                           preferred_element_type=_F32)


def _cumsum_rows(x, tri_bf):
    hi = x.astype(_BF)
    r1 = x - hi.astype(_F32)
    mid = r1.astype(_BF)
    lo = (r1 - mid.astype(_F32)).astype(_BF)
    acc = jnp.dot(tri_bf, hi, preferred_element_type=_F32)
    acc = acc + jnp.dot(tri_bf, mid, preferred_element_type=_F32)
    return acc + jnp.dot(tri_bf, lo, preferred_element_type=_F32)


def _tril_ones(n):
    r = lax.broadcasted_iota(jnp.int32, (n, n), 0)
    c = lax.broadcasted_iota(jnp.int32, (n, n), 1)
    return (r >= c).astype(_BF)


def _silu(x):
    return x * jax.nn.sigmoid(x)


def _rmsnorm_kernel(x_ref, w_ref, o_ref):
    x = x_ref[...]
    ms = jnp.mean(x * x, axis=-1, keepdims=True)
    o_ref[...] = (x * lax.rsqrt(ms + EPS) * w_ref[...]).astype(o_ref.dtype)


def _rmsnorm(x, w, out_dtype, tm=512):
    t, d = x.shape
    return pl.pallas_call(
        _rmsnorm_kernel,
        grid=(t // tm,),
        in_specs=[pl.BlockSpec((tm, d), lambda i: (i, 0)), pl.BlockSpec((1, d), lambda i: (0, 0))],
        out_specs=pl.BlockSpec((tm, d), lambda i: (i, 0)),
        out_shape=jax.ShapeDtypeStruct((t, d), out_dtype),
        compiler_params=pltpu.CompilerParams(dimension_semantics=("parallel",),
                                             vmem_limit_bytes=VMEM_LIMIT),
        name="rmsnorm",
    )(x, w.reshape(1, d))


def _rmsnorm_proj_kernel(x_ref, w_ref, p_ref, o_ref, y_ref):
    x = x_ref[...]
    ms = jnp.mean(x * x, axis=-1, keepdims=True)
    h = (x * lax.rsqrt(ms + EPS) * w_ref[...]).astype(o_ref.dtype)
    o_ref[...] = h
    y_ref[...] = jnp.dot(h, p_ref[...], preferred_element_type=_F32)


def _rmsnorm_proj(x, w, p, out_dtype, tm=512):
    t, d = x.shape
    tm = min(tm, t)
    return pl.pallas_call(
        _rmsnorm_proj_kernel,
        grid=(t // tm,),
        in_specs=[pl.BlockSpec((tm, d), lambda i: (i, 0)), pl.BlockSpec((1, d), lambda i: (0, 0)),
                  pl.BlockSpec((d, LANES), lambda i: (0, 0))],
        out_specs=[pl.BlockSpec((tm, d), lambda i: (i, 0)), pl.BlockSpec((tm, LANES), lambda i: (i, 0))],
        out_shape=[jax.ShapeDtypeStruct((t, d), out_dtype), jax.ShapeDtypeStruct((t, LANES), _F32)],
        compiler_params=pltpu.CompilerParams(dimension_semantics=("parallel",),
                                             vmem_limit_bytes=VMEM_LIMIT),
        name="rmsnorm_gates",
    )(x, w.reshape(1, d), p)


def _mm_kernel(a_ref, b_ref, o_ref):
    o_ref[...] = jnp.dot(a_ref[...], b_ref[...], preferred_element_type=_F32).astype(o_ref.dtype)


def _mm_res_kernel(a_ref, b_ref, r_ref, o_ref):
    acc = jnp.dot(a_ref[...], b_ref[...], preferred_element_type=_F32)
    o_ref[...] = (acc + r_ref[...]).astype(o_ref.dtype)


def _matmul(a, b, *, tm, tn, out_dtype, res=None, name="matmul"):
    m, k = a.shape
    _, n = b.shape
    tm = min(tm, m)
    in_specs = [pl.BlockSpec((tm, k), lambda i, j: (i, 0)), pl.BlockSpec((k, tn), lambda i, j: (0, j))]
    args = [a, b]
    kern = _mm_kernel
    if res is not None:
        in_specs.append(pl.BlockSpec((tm, tn), lambda i, j: (i, j)))
        args.append(res)
        kern = _mm_res_kernel
    return pl.pallas_call(
        kern,
        grid=(m // tm, n // tn),
        in_specs=in_specs,
        out_specs=pl.BlockSpec((tm, tn), lambda i, j: (i, j)),
        out_shape=jax.ShapeDtypeStruct((m, n), out_dtype),
        compiler_params=pltpu.CompilerParams(dimension_semantics=("parallel", "arbitrary"),
                                             vmem_limit_bytes=VMEM_LIMIT),
        name=name,
    )(*args)


def _mm_wcast_kernel(a_ref, w_ref, o_ref, wbf_ref):
    @pl.when(pl.program_id(1) == 0)
    def _():
        wbf_ref[...] = w_ref[...].astype(_BF)

    o_ref[...] = jnp.dot(a_ref[...], wbf_ref[...], preferred_element_type=_F32).astype(o_ref.dtype)


def _mm_wcast_t_kernel(a_ref, wt_ref, o_ref, wbf_ref):
    @pl.when(pl.program_id(1) == 0)
    def _():
        wbf_ref[...] = wt_ref[...].astype(_BF)

    o_ref[...] = lax.dot_general(a_ref[...], wbf_ref[...], (((1,), (1,)), ((), ())),
                                 preferred_element_type=_F32).astype(o_ref.dtype)


def _matmul_wcast(a, w, layer, *, n_cols, tm, tn, out_dtype, name, transposed=False):
    m, k = a.shape
    tm = min(tm, m)
    if transposed:
        kern, w_spec, w_scr = (_mm_wcast_t_kernel, pl.BlockSpec((None, tn, k), lambda j, i: (layer, j, 0)),
                               pltpu.VMEM((tn, k), _BF))
    else:
        kern, w_spec, w_scr = (_mm_wcast_kernel, pl.BlockSpec((None, k, tn), lambda j, i: (layer, 0, j)),
                               pltpu.VMEM((k, tn), _BF))
    return pl.pallas_call(
        kern,
        grid=(n_cols // tn, m // tm),
        in_specs=[pl.BlockSpec((tm, k), lambda j, i: (i, 0)), w_spec],
        out_specs=pl.BlockSpec((tm, tn), lambda j, i: (i, j)),
        out_shape=jax.ShapeDtypeStruct((m, n_cols), out_dtype),
        scratch_shapes=[w_scr],
        compiler_params=pltpu.CompilerParams(dimension_semantics=("parallel", "arbitrary"),
                                             vmem_limit_bytes=VMEM_LIMIT),
        name=name,
    )(a, w)


def _hgrn_kernel(hq_ref, hf_ref, hi_ref, hg_ref, lb_ref, nw_ref, o_ref,
                 st_ref, q_s, k_s, b_s, *, heads):
    @pl.when(pl.program_id(1) == 0)
    def _():
        st_ref[...] = jnp.zeros_like(st_ref)

    for ci in range(hq_ref.shape[0] // CHUNK):
        rs = pl.ds(ci * CHUNK, CHUNK)
        _hgrn_chunk(hq_ref.at[rs, :], hf_ref.at[rs, :], hi_ref.at[rs, :], hg_ref.at[rs, :], lb_ref, nw_ref,
                    o_ref.at[rs, :], st_ref, q_s, k_s, b_s, heads=heads)


def _hgrn_chunk(hq_ref, hf_ref, hi_ref, hg_ref, lb_ref, nw_ref, o_ref,
                st_ref, q_s, k_s, b_s, *, heads):
    lb = lb_ref[...]
    f = lb + (1.0 - lb) * jax.nn.sigmoid(hf_ref[...])
    b_all = _cumsum_rows(jnp.log(f) * LOG2E, _tril_ones(CHUNK))
    q_all = _silu(hq_ref[...]) * (HEAD_DIM ** -0.5)
    k_all = 1.0 - f
    for h in range(heads):
        sl = slice(h * HEAD_DIM, (h + 1) * HEAD_DIM)
        q_s[h] = q_all[:, sl]
        k_s[h] = k_all[:, sl]
        b_s[h] = b_all[:, sl]

    nw = nw_ref[...]
    lane = lax.broadcasted_iota(jnp.int32, (SUB, CHUNK), 1)
    row = lax.broadcasted_iota(jnp.int32, (SUB, CHUNK), 0)
    r64 = lax.broadcasted_iota(jnp.int32, (CHUNK, CHUNK), 0)
    c64 = lax.broadcasted_iota(jnp.int32, (CHUNK, CHUNK), 1)
    _, _, level_masks = _doubling_masks(r64, c64)
    sub_log = SUB.bit_length() - 1
    level_masks = level_masks[sub_log - 1:]
    lower = r64 > c64

    v_bf, off, o_inter = [], [], []
    for h in range(heads):
        sl = slice(h * HEAD_DIM, (h + 1) * HEAD_DIM)
        q = q_s[h]
        k = k_s[h]
        b = b_s[h]
        v = hi_ref[:, sl].astype(_BF)
        st = st_ref[h]
        b_last = b_s[h, CHUNK - 1:CHUNK, :]
        acc = jnp.zeros((CHUNK, CHUNK), _F32)
        s = SUB
        for m in level_masks:
            ref = jnp.concatenate(
                [jnp.broadcast_to(b_s[h, g + s - 1:g + s, :], (2 * s, HEAD_DIM))
                 for g in range(0, CHUNK, 2 * s)], axis=0)
            ql = q * jnp.exp2(b - ref)
            kl = k * jnp.exp2(ref - b)
            acc = jnp.where(m & lower, _dot_nt(ql, kl), acc)
            s *= 2
        off.append(acc)
        o_inter.append(_dot_nt(q * jnp.exp2(b), st))
        kd = k * jnp.exp2(b_last - b)
        st_ref[h] = st * jnp.exp2(b_last) + _dot_tn(v, kd)
        v_bf.append(v)

    attn = []
    for h in range(heads):
        rows = []
        for blk in range(CHUNK // SUB):
            lo = blk * SUB
            qb = q_s[h, lo:lo + SUB, :]
            bb = b_s[h, lo:lo + SUB, :]
            acc = jnp.zeros((SUB, CHUNK), _F32)
            for s in range(SUB):
                e = jnp.exp2(bb - b_s[h, lo + s:lo + s + 1, :])
                p = qb * e * k_s[h, lo + s:lo + s + 1, :]
                acc = jnp.where(lane == lo + s, jnp.sum(p, axis=-1, keepdims=True), acc)
            in_block = (lane >= lo) & (row + lo >= lane)
            rows.append(jnp.where(in_block, acc, off[h][lo:lo + SUB, :]))
        attn.append(jnp.concatenate(rows, axis=0).astype(_BF))

    o_intra = [_dot(attn[h], v_bf[h]) for h in range(heads)]
    for h in range(heads):
        sl = slice(h * HEAD_DIM, (h + 1) * HEAD_DIM)
        o = o_inter[h] + o_intra[h]
        ms = jnp.mean(o * o, axis=-1, keepdims=True)
        y = o * lax.rsqrt(ms + EPS) * nw * _silu(hg_ref[:, sl])
        o_ref[:, sl] = y.astype(o_ref.dtype)


def _hgrn(proj, lb, nw, *, batch, seq, col0, out_dtype=_BF):
    width = lb.shape[-1]
    heads = width // HEAD_DIM
    rows = HGRN_STEP_CHUNKS * CHUNK
    nc = seq // rows
    cb = col0 // width

    def spec(off):
        return pl.BlockSpec((rows, width), lambda bi, ci, off=off: (bi * nc + ci, cb + off))

    return pl.pallas_call(
        functools.partial(_hgrn_kernel, heads=heads),
        grid=(batch, nc),
        in_specs=[spec(0), spec(1), spec(2), spec(3),
                  pl.BlockSpec((1, width), lambda bi, ci: (0, 0)),
                  pl.BlockSpec((1, HEAD_DIM), lambda bi, ci: (0, 0))],
        out_specs=pl.BlockSpec((rows, width), lambda bi, ci: (bi * nc + ci, 0)),
        out_shape=jax.ShapeDtypeStruct((batch * seq, width), out_dtype),
        scratch_shapes=[pltpu.VMEM((heads, HEAD_DIM, HEAD_DIM), _F32),
                        pltpu.VMEM((heads, CHUNK, HEAD_DIM), _F32),
                        pltpu.VMEM((heads, CHUNK, HEAD_DIM), _F32),
                        pltpu.VMEM((heads, CHUNK, HEAD_DIM), _F32)],
        compiler_params=pltpu.CompilerParams(dimension_semantics=("parallel", "arbitrary"),
                                             vmem_limit_bytes=VMEM_LIMIT),
        name="hgrn2",
    )(proj, proj, proj, proj, lb.reshape(1, width), nw.reshape(1, HEAD_DIM))


def _doubling_masks(r, c):
    eye = (r == c).astype(_F32)
    pair = (r >> 1) == (c >> 1)
    levels = []
    k = 1
    while (2 << k) <= CHUNK:
        levels.append(((r >> (k + 1)) == (c >> (k + 1))) & ((r >> k) != (c >> k)))
        k += 1
    return eye, pair, levels


def _unit_lower_inverses(mats, masks):
    eye, pair, levels = masks
    ts = [eye - jnp.where(pair, a, 0.0) for a in mats]
    for m in levels:
        xs = [_dot(jnp.where(m, a, 0.0), t) for a, t in zip(mats, ts)]
        ts = [t - _dot(t, x) for t, x in zip(ts, xs)]
    return ts


def _gdn_kernel(qkv_ref, z_ref, ba_ref, cw_ref, alog_ref, dtb_ref, nw_ref, o_ref,
                st_ref, stage_ref, *, k_heads, v_heads):
    c = pl.program_id(1)
    kw = k_heads * HEAD_DIM
    rep = v_heads // k_heads

    @pl.when(c == 0)
    def _():
        st_ref[...] = jnp.zeros_like(st_ref)
        stage_ref[0:SUBLANES, :] = jnp.zeros((SUBLANES, stage_ref.shape[1]), _F32)

    rows = qkv_ref.shape[0]
    n_chunks = rows // CHUNK
    x = qkv_ref[...]
    stage_ref[SUBLANES:SUBLANES + rows, :] = x
    cw = cw_ref[...]
    y = x * cw[GDN_CONV_WIDTH - 1:GDN_CONV_WIDTH, :]
    for j in range(GDN_CONV_WIDTH - 1):
        sh = GDN_CONV_WIDTH - 1 - j
        y = y + stage_ref[SUBLANES - sh:SUBLANES - sh + rows, :] * cw[j:j + 1, :]
    stage_ref[0:SUBLANES, :] = x[rows - SUBLANES:rows, :]
    act = _silu(y)

    r = lax.broadcasted_iota(jnp.int32, (CHUNK, CHUNK), 0)
    cc = lax.broadcasted_iota(jnp.int32, (CHUNK, CHUNK), 1)
    nw = nw_ref[...]
    masks = _doubling_masks(r, cc)
    tri = _tril_ones(CHUNK)
    hs = range(v_heads)

    qn, kn, vh, gcol, glast, qkl, amat, brow, grow = [], [], [], [], [], [], [], [], []
    for ci in range(n_chunks):
        rs = slice(ci * CHUNK, (ci + 1) * CHUNK)
        ba = ba_ref[rs, :]
        beta_all = jax.nn.sigmoid(ba)
        xs = ba + dtb_ref[...]
        softplus = jnp.maximum(xs, 0.0) + jnp.log(1.0 + jnp.exp(-jnp.abs(xs)))
        gc_all = _cumsum_rows(-jnp.exp(alog_ref[...]) * softplus, tri)
        beta_t = beta_all.T
        gc_t = gc_all.T
        q_c, k_c = [], []
        for kh in range(k_heads):
            qh = act[rs, kh * HEAD_DIM:(kh + 1) * HEAD_DIM]
            kk_ = act[rs, kw + kh * HEAD_DIM:kw + (kh + 1) * HEAD_DIM]
            qh = qh * lax.rsqrt(jnp.sum(qh * qh, axis=-1, keepdims=True) + EPS) * (HEAD_DIM ** -0.5)
            kk_ = kk_ * lax.rsqrt(jnp.sum(kk_ * kk_, axis=-1, keepdims=True) + EPS)
            q_c.append(qh.astype(_BF))
            k_c.append(kk_.astype(_BF))
        kkt = [_dot_nt(k_, k_) for k_ in k_c]
        qkt = [_dot_nt(q_, k_) for q_, k_ in zip(q_c, k_c)]
        for h in hs:
            bcol = beta_all[:, h:h + 1]
            gc = gc_all[:, v_heads + h:v_heads + h + 1]
            gr = gc_t[v_heads + h:v_heads + h + 1, :]
            dec = jnp.where(r >= cc, jnp.exp(gc - gr), 0.0)
            qn.append(q_c[h // rep])
            kn.append(k_c[h // rep])
            vh.append(act[rs, 2 * kw + h * HEAD_DIM:2 * kw + (h + 1) * HEAD_DIM].astype(_BF))
            gcol.append(gc)
            glast.append(gc[CHUNK - 1:CHUNK, :])
            brow.append(beta_t[h:h + 1, :])
            grow.append(gr)
            qkl.append((qkt[h // rep] * dec).astype(_BF))
            amat.append(jnp.where(r > cc, kkt[h // rep] * (bcol * dec), 0.0))
    n = len(amat)
    tinv = _unit_lower_inverses(amat, masks)
    tb = [tinv[i] * brow[i] for i in range(n)]
    u = [_dot(tb[i], vh[i]) for i in range(n)]
    w = [_dot(tb[i] * jnp.exp(grow[i]), kn[i]) for i in range(n)]

    st = [st_ref[h] for h in hs]
    for ci in range(n_chunks):
        rs = slice(ci * CHUNK, (ci + 1) * CHUNK)
        ix = [ci * v_heads + h for h in hs]
        st_bf = [s_.astype(_BF) for s_ in st]
        ws = [_dot(w[ix[h]], st_bf[h]) for h in hs]
        qs = [_dot(qn[ix[h]], st_bf[h]) for h in hs]
        v_new = [u[ix[h]] - ws[h] for h in hs]
        o_intra = [_dot(qkl[ix[h]], v_new[h]) for h in hs]
        kv = [_dot_tn(kn[ix[h]], v_new[h] * jnp.exp(glast[ix[h]] - gcol[ix[h]])) for h in hs]
        st = [st[h] * jnp.exp(glast[ix[h]]) + kv[h] for h in hs]
        for h in hs:
            o = jnp.exp(gcol[ix[h]]) * qs[h] + o_intra[h]
            ms = jnp.mean(o * o, axis=-1, keepdims=True)
            zh = z_ref[rs, h * HEAD_DIM:(h + 1) * HEAD_DIM]
            yh = o * lax.rsqrt(ms + EPS) * nw * _silu(zh)
            o_ref[rs, h * HEAD_DIM:(h + 1) * HEAD_DIM] = yh.astype(o_ref.dtype)
    for h in hs:
        st_ref[h] = st[h]


def _gdn(proj, ba, conv_w, alog_pad, dtb_pad, nw, *, batch, seq, qkv_col0, z_col0, k_heads, v_heads,
         out_dtype=_BF):
    qkv_w = (2 * k_heads + v_heads) * HEAD_DIM
    vw = v_heads * HEAD_DIM
    rows = GDN_STEP_CHUNKS * CHUNK
    nc = seq // rows
    qb = qkv_col0 // qkv_w
    zb = z_col0 // vw
    return pl.pallas_call(
        functools.partial(_gdn_kernel, k_heads=k_heads, v_heads=v_heads),
        grid=(batch, nc),
        in_specs=[pl.BlockSpec((rows, qkv_w), lambda bi, ci: (bi * nc + ci, qb)),
                  pl.BlockSpec((rows, vw), lambda bi, ci: (bi * nc + ci, zb)),
                  pl.BlockSpec((rows, LANES), lambda bi, ci: (bi * nc + ci, 0)),
                  pl.BlockSpec((GDN_CONV_WIDTH, qkv_w), lambda bi, ci: (0, 0)),
                  pl.BlockSpec((1, LANES), lambda bi, ci: (0, 0)),
                  pl.BlockSpec((1, LANES), lambda bi, ci: (0, 0)),
                  pl.BlockSpec((1, HEAD_DIM), lambda bi, ci: (0, 0))],
        out_specs=pl.BlockSpec((rows, vw), lambda bi, ci: (bi * nc + ci, 0)),
        out_shape=jax.ShapeDtypeStruct((batch * seq, vw), out_dtype),
        scratch_shapes=[pltpu.VMEM((v_heads, HEAD_DIM, HEAD_DIM), _F32),
                        pltpu.VMEM((SUBLANES + rows, qkv_w), _F32)],
        compiler_params=pltpu.CompilerParams(dimension_semantics=("parallel", "arbitrary"),
                                             vmem_limit_bytes=VMEM_LIMIT),
        name="gated_delta",
    )(proj, proj, ba, conv_w, alog_pad, dtb_pad, nw.reshape(1, HEAD_DIM))


def _mm2_res_kernel(a1_ref, a2_ref, w_ref, r_ref, o_ref, wbf_ref):
    @pl.when(pl.program_id(1) == 0)
    def _():
        wbf_ref[...] = w_ref[...].astype(_BF)

    k1 = a1_ref.shape[1]
    acc = jnp.dot(a1_ref[...], wbf_ref[0:k1, :], preferred_element_type=_F32)
    acc = acc + jnp.dot(a2_ref[...], wbf_ref[k1:, :], preferred_element_type=_F32)
    o_ref[...] = (acc + r_ref[...]).astype(o_ref.dtype)


def _out_proj(a1, a2, w, layer, res, *, tm, tn):
    m, k1 = a1.shape
    _, k2 = a2.shape
    n = w.shape[2]
    tm = min(tm, m)
    return pl.pallas_call(
        _mm2_res_kernel,
        grid=(n // tn, m // tm),
        in_specs=[pl.BlockSpec((tm, k1), lambda j, i: (i, 0)),
                  pl.BlockSpec((tm, k2), lambda j, i: (i, 0)),
                  pl.BlockSpec((None, k1 + k2, tn), lambda j, i: (layer, 0, j)),
                  pl.BlockSpec((tm, tn), lambda j, i: (i, j))],
        out_specs=pl.BlockSpec((tm, tn), lambda j, i: (i, j)),
        out_shape=jax.ShapeDtypeStruct((m, n), _F32),
        scratch_shapes=[pltpu.VMEM((k1 + k2, tn), _BF)],
        compiler_params=pltpu.CompilerParams(dimension_semantics=("parallel", "arbitrary"),
                                             vmem_limit_bytes=VMEM_LIMIT),
        name="out_proj",
    )(a1, a2, w, res)


def _ffn_in_kernel(a_ref, wg_ref, wu_ref, cw_ref, cb_ref, o_ref, wg_bf, wu_bf, stage_ref, *, seq_tiles):
    i = pl.program_id(1)
    rows = a_ref.shape[0]

    @pl.when(i == 0)
    def _():
        wg_bf[...] = wg_ref[...].astype(_BF)
        wu_bf[...] = wu_ref[...].astype(_BF)

    @pl.when(i % seq_tiles == 0)
    def _():
        stage_ref[0:SUBLANES, :] = jnp.zeros((SUBLANES, stage_ref.shape[1]), _F32)

    a = a_ref[...]
    g = jnp.dot(a, wg_bf[...], preferred_element_type=_F32)
    u = jnp.dot(a, wu_bf[...], preferred_element_type=_F32)
    stage_ref[SUBLANES:SUBLANES + rows, :] = g
    cw = cw_ref[...]
    y = g * cw[FFN_CONV_WIDTH - 1:FFN_CONV_WIDTH, :] + cb_ref[...]
    for j in range(FFN_CONV_WIDTH - 1):
        sh = FFN_CONV_WIDTH - 1 - j
        y = y + stage_ref[SUBLANES - sh:SUBLANES - sh + rows, :] * cw[j:j + 1, :]
    stage_ref[0:SUBLANES, :] = g[rows - SUBLANES:rows, :]
    o_ref[...] = (_silu(y) * u).astype(o_ref.dtype)


def _ffn_in(a, w, layer, cw, cb, *, seq, tm, tn):
    m, k = a.shape
    d_ff = cb.shape[-1]
    tm = min(tm, seq)
    ncol = d_ff // tn
    return pl.pallas_call(
        functools.partial(_ffn_in_kernel, seq_tiles=seq // tm),
        grid=(ncol, m // tm),
        in_specs=[pl.BlockSpec((tm, k), lambda j, i: (i, 0)),
                  pl.BlockSpec((None, k, tn), lambda j, i: (layer, 0, j)),
                  pl.BlockSpec((None, k, tn), lambda j, i: (layer, 0, ncol + j)),
                  pl.BlockSpec((FFN_CONV_WIDTH, tn), lambda j, i: (0, j)),
                  pl.BlockSpec((1, tn), lambda j, i: (0, j))],
        out_specs=pl.BlockSpec((tm, tn), lambda j, i: (i, j)),
        out_shape=jax.ShapeDtypeStruct((m, d_ff), _BF),
        scratch_shapes=[pltpu.VMEM((k, tn), _BF), pltpu.VMEM((k, tn), _BF),
                        pltpu.VMEM((SUBLANES + tm, tn), _F32)],
        compiler_params=pltpu.CompilerParams(dimension_semantics=("parallel", "arbitrary"),
                                             vmem_limit_bytes=VMEM_LIMIT),
        name="ffn_in",
    )(a, w, w, cw, cb.reshape(1, d_ff))


def kernel(x, norm_mix_w, w_in, hgrn_lb_raw, hgrn_norm_w, gdn_conv_w, gdn_A_log, gdn_dt_bias, gdn_norm_w,
           w_out, norm_ffn_w, w_ffn_in, ffn_conv_w, ffn_conv_b, w_ffn_out, final_norm_w):
    batch, seq, d = x.shape
    depth = w_in.shape[0]
    v_heads = gdn_A_log.shape[1]
    gdn_w = v_heads * HEAD_DIM
    k_heads = (gdn_conv_w.shape[2] - gdn_w) // (2 * HEAD_DIM)
    qkv_w = gdn_conv_w.shape[2]
    hgrn_w = hgrn_lb_raw.shape[1]
    d_ff = ffn_conv_b.shape[1]
    main_w = 4 * hgrn_w + qkv_w + gdn_w
    t = batch * seq

    lb_all = jnp.cumsum(jax.nn.softmax(hgrn_lb_raw.astype(_F32), axis=0), axis=0)

    w_in_t = jnp.swapaxes(w_in, 1, 2)

    xt = x.reshape(t, d)
    for l in range(depth):
        w_gate = jnp.pad(w_in[l, :, main_w:], ((0, 0), (0, LANES - 2 * v_heads))).astype(_BF)
        pad_lo = jnp.zeros((v_heads,), _F32)
        pad_hi = jnp.zeros((LANES - 2 * v_heads,), _F32)
        alog_pad = jnp.concatenate([pad_lo, gdn_A_log[l].astype(_F32), pad_hi]).reshape(1, LANES)
        dtb_pad = jnp.concatenate([pad_lo, gdn_dt_bias[l].astype(_F32), pad_hi]).reshape(1, LANES)

        h, ba = _rmsnorm_proj(xt, norm_mix_w[l], w_gate, _BF)
        tm, tn = TILES["in_proj"]
        proj = _matmul_wcast(h, w_in_t, l, n_cols=main_w, tm=tm, tn=tn, out_dtype=_F32, name="in_proj",
                             transposed=True)

        o_h = _hgrn(proj, lb_all[l], hgrn_norm_w[l], batch=batch, seq=seq, col0=0)
        o_g = _gdn(proj, ba, gdn_conv_w[l], alog_pad, dtb_pad, gdn_norm_w[l], batch=batch, seq=seq,
                   qkv_col0=4 * hgrn_w, z_col0=4 * hgrn_w + qkv_w, k_heads=k_heads, v_heads=v_heads)
        tm, tn = TILES["out_proj"]
        xt = _out_proj(o_h, o_g, w_out, l, xt, tm=tm, tn=tn)

        h2 = _rmsnorm(xt, norm_ffn_w[l], _BF)
        tm, tn = TILES["ffn_in"]
        act = _ffn_in(h2, w_ffn_in, l, ffn_conv_w[l], ffn_conv_b[l], seq=seq, tm=tm, tn=tn)
        tm, tn = TILES["ffn_out"]
        xt = _matmul(act, w_ffn_out[l].astype(_BF), tm=tm, tn=tn, out_dtype=_F32, res=xt, name="ffn_out")

    out = _rmsnorm(xt, final_norm_w, x.dtype)
    return out.reshape(batch, seq, d)
```
